```python
import jax, jax.numpy as jnp
from jax import lax
import numpy as np

D_MODEL = 2048
BATCH = 2
SEQ = 16384
DEPTH = 2

CHUNK = 64
Q_BLOCK = 128
EPS = 1e-6

MLA_HEADS = 8
MLA_NOPE = 128
MLA_ROPE = 64
MLA_V = 128
MLA_Q_LORA = 512
MLA_KV_LORA = 512
ROPE_THETA = 10000.0
MLA_WIDTH = MLA_HEADS * MLA_V

POOL_WINDOWS = (2, 4, 8, 16)
POOL_GROUPS = 4
POOL_GROUP_W = 256
POOL_WIDTH = POOL_GROUPS * POOL_GROUP_W

GLA_HEADS = 4
GLA_DK = 128
GLA_DV = 256
GLA_GATE_RANK = 16
GLA_GATE_TAU = 16.0
GLA_QK_WIDTH = GLA_HEADS * GLA_DK
GLA_WIDTH = GLA_HEADS * GLA_DV

N_BRANCH = 3
BRANCH_WIDTH = 1024

SPLIT_SIZES = (MLA_Q_LORA, MLA_KV_LORA, MLA_ROPE, MLA_WIDTH,
               POOL_WIDTH, POOL_WIDTH,
               GLA_QK_WIDTH, GLA_QK_WIDTH, GLA_WIDTH,
               GLA_GATE_RANK, GLA_WIDTH,
               N_BRANCH * D_MODEL)
N_IN = sum(SPLIT_SIZES)

kernel_name = "hybrid_mla_pool_gla_gated_merge"


def _split_points():
    pts, acc = [], 0
    for s in SPLIT_SIZES[:-1]:
        acc += s
        pts.append(acc)
    return pts


def rmsnorm(x, g):
    xf = x.astype(jnp.float32)
    y = xf * lax.rsqrt(jnp.mean(xf * xf, axis=-1, keepdims=True) + EPS)
    return (y * g.astype(jnp.float32)).astype(x.dtype)


def rope_tables(positions):
    inv = 1.0 / (ROPE_THETA ** (jnp.arange(0, MLA_ROPE, 2, dtype=jnp.float32) / MLA_ROPE))
    ang = positions.astype(jnp.float32)[..., None] * inv
    return jnp.cos(ang)[:, :, None, :], jnp.sin(ang)[:, :, None, :]


def apply_rope(x, cos, sin):
    xf = x.astype(jnp.float32)
    x1, x2 = jnp.split(xf, 2, axis=-1)
    return jnp.concatenate([x1 * cos - x2 * sin, x2 * cos + x1 * sin], axis=-1).astype(x.dtype)


def block_causal_attention(q, k, v):
    B, S, H, Dq = q.shape
    Dv = v.shape[-1]
    nb = S // Q_BLOCK
    scale = Dq ** -0.5
    qb = q.reshape(B, nb, Q_BLOCK, H, Dq).transpose(1, 0, 2, 3, 4)
    k_chunk = jnp.arange(S) // CHUNK

    def one_block(args):
        qi, i = args
        q_chunk = (i * Q_BLOCK + jnp.arange(Q_BLOCK)) // CHUNK
        mask = k_chunk[None, :] <= q_chunk[:, None]
        s = jnp.einsum('bqhd,bkhd->bhqk', qi, k).astype(jnp.float32) * scale
        s = jnp.where(mask[None, None], s, -1e30)
        p = jax.nn.softmax(s, axis=-1).astype(v.dtype)
        return jnp.einsum('bhqk,bkhv->bqhv', p, v)

    out = lax.map(one_block, (qb, jnp.arange(nb)))
    return out.transpose(1, 0, 2, 3, 4).reshape(B, S, H, Dv)


def mla_branch(c_q, c_kv, k_rope_raw, q_norm_g, kv_norm_g, w_uq, w_ukv, cos, sin):
    B, S, _ = c_q.shape
    q = (rmsnorm(c_q, q_norm_g) @ w_uq).reshape(B, S, MLA_HEADS, MLA_NOPE + MLA_ROPE)
    q_nope, q_rope = q[..., :MLA_NOPE], q[..., MLA_NOPE:]
    q = jnp.concatenate([q_nope, apply_rope(q_rope, cos, sin)], axis=-1)
    kv = (rmsnorm(c_kv, kv_norm_g) @ w_ukv).reshape(B, S, MLA_HEADS, MLA_NOPE + MLA_V)
    k_nope, v = kv[..., :MLA_NOPE], kv[..., MLA_NOPE:]
    k_rope = apply_rope(k_rope_raw[:, :, None, :], cos, sin)
    k = jnp.concatenate([k_nope, jnp.broadcast_to(k_rope, (B, S, MLA_HEADS, MLA_ROPE))], axis=-1)
    o = block_causal_attention(q, k, v)
    return o.reshape(B, S, MLA_WIDTH)


def pool_branch(u, pool_w, pool_scale):
    B, S, _ = u.shape
    uf = u.astype(jnp.float32).reshape(B, S, POOL_GROUPS, POOL_GROUP_W)
    cs = jnp.cumsum(uf, axis=1)
    t = jnp.arange(S)
    mixed = []
    for g, w in enumerate(POOL_WINDOWS):
        csg = cs[:, :, g]
        shifted = jnp.pad(csg, ((0, 0), (w, 0), (0, 0)))[:, :S]
        count = jnp.minimum(t + 1, w).astype(jnp.float32)[None, :, None]
        mixed.append((csg - shifted) / count - uf[:, :, g])
    m = jnp.stack(mixed, axis=2).astype(u.dtype)
    y = jnp.einsum('bsgc,gcd->bsgd', m, pool_w).reshape(B, S, POOL_WIDTH)
    return y * pool_scale


def gla_branch(q, k, v, gate_lr, w_gate, b_gate, norm_g):
    B, S, _ = q.shape
    N, L, H = S // CHUNK, CHUNK, GLA_HEADS
    f32 = jnp.float32
    qc = q.astype(f32).reshape(B, N, L, H, GLA_DK) * (GLA_DK ** -0.5)
    kc = k.astype(f32).reshape(B, N, L, H, GLA_DK)
    vc = v.astype(f32).reshape(B, N, L, H, GLA_DV)
    log_a = jax.nn.log_sigmoid((gate_lr @ w_gate + b_gate).astype(f32)) / GLA_GATE_TAU
    log_a = log_a.reshape(B, N, L, H, GLA_DK)
    bcum = jnp.cumsum(log_a, axis=2)
    b_last = bcum[:, :, -1:]
    q_dec = qc * jnp.exp(bcum)
    k_dec = kc * jnp.exp(-bcum)
    tril = jnp.tril(jnp.ones((L, L), dtype=bool))
    att = jnp.einsum('bnihd,bnjhd->bnhij', q_dec, k_dec)
    att = jnp.where(tril[None, None, None], att, 0.0)
    o_intra = jnp.einsum('bnhij,bnjhv->bnihv', att, vc)
    k_end = kc * jnp.exp(b_last - bcum)
    U = jnp.einsum('bnjhd,bnjhv->nbhdv', k_end, vc)
    decay = jnp.exp(b_last[:, :, 0]).transpose(1, 0, 2, 3)

    def step(state, inp):
        d, u_c = inp
        return d[..., None] * state + u_c, state

    _, s_prev = lax.scan(step, jnp.zeros((B, H, GLA_DK, GLA_DV), f32), (decay, U))
    o_inter = jnp.einsum('bnihd,nbhdv->bnihv', q_dec, s_prev)
    o = (o_intra + o_inter).reshape(B, S, H, GLA_DV)
    o = o * lax.rsqrt(jnp.mean(o * o, axis=-1, keepdims=True) + EPS)
    return (o.reshape(B, S, GLA_WIDTH) * norm_g.astype(f32)).astype(q.dtype)


def hybrid_layer(x, cos, sin, norm_g, w_in, b_merge, mla_q_norm, mla_kv_norm, mla_w_uq, mla_w_ukv,
                 pool_w, pool_scale, gla_w_gate, gla_b_gate, gla_norm, w_branch, w_out):
    B, S, D = x.shape
    h = rmsnorm(x, norm_g)
    proj = h @ w_in
    (c_q, c_kv, k_rope, z_a, u_b, z_b, q_c, k_c, v_c, gate_lr, z_c, gates) = jnp.split(proj, _split_points(), axis=-1)
    y_a = mla_branch(c_q, c_kv, k_rope, mla_q_norm, mla_kv_norm, mla_w_uq, mla_w_ukv, cos, sin) * jax.nn.silu(z_a)
    y_b = pool_branch(u_b, pool_w, pool_scale) * jax.nn.silu(z_b)
    y_c = gla_branch(q_c, k_c, v_c, gate_lr, gla_w_gate, gla_b_gate, gla_norm) * jax.nn.silu(z_c)
    ys = jnp.stack([y_a, y_b, y_c], axis=2)
    p = jnp.einsum('bsnw,nwd->bsnd', ys, w_branch)
    g = jax.nn.sigmoid(gates.reshape(B, S, N_BRANCH, D) + b_merge)
    merged = jnp.sum(g * p, axis=2)
    return x + merged @ w_out


def setup_inputs(seed: int = 0) -> dict:
    key = jax.random.key(seed)
    ks = jax.random.split(key, 16)
    f32 = jnp.float32

    def nrm(k, shape, fan_in):
        return jax.random.normal(k, shape, f32) * (fan_in ** -0.5)

    def gain(k, shape):
        return 1.0 + 0.02 * jax.random.normal(k, shape, f32)

    return {
        "x": jax.random.normal(ks[0], (BATCH, SEQ, D_MODEL), f32),
        "positions": jnp.broadcast_to(jnp.arange(SEQ, dtype=jnp.int32)[None, :], (BATCH, SEQ)),
        "norm_g": gain(ks[1], (DEPTH, D_MODEL)),
        "w_in": nrm(ks[2], (DEPTH, D_MODEL, N_IN), D_MODEL),
        "b_merge": 0.02 * jax.random.normal(ks[3], (DEPTH, N_BRANCH, D_MODEL), f32),
        "mla_q_norm": gain(ks[4], (DEPTH, MLA_Q_LORA)),
        "mla_kv_norm": gain(ks[5], (DEPTH, MLA_KV_LORA)),
        "mla_w_uq": nrm(ks[6], (DEPTH, MLA_Q_LORA, MLA_HEADS * (MLA_NOPE + MLA_ROPE)), MLA_Q_LORA),
        "mla_w_ukv": nrm(ks[7], (DEPTH, MLA_KV_LORA, MLA_HEADS * (MLA_NOPE + MLA_V)), MLA_KV_LORA),
        "pool_w": nrm(ks[8], (DEPTH, POOL_GROUPS, POOL_GROUP_W, POOL_GROUP_W), POOL_GROUP_W),
        "pool_scale": gain(ks[9], (DEPTH, POOL_WIDTH)),
        "gla_w_gate": nrm(ks[10], (DEPTH, GLA_GATE_RANK, GLA_QK_WIDTH), GLA_GATE_RANK),
        "gla_b_gate": 0.02 * jax.random.normal(ks[11], (DEPTH, GLA_QK_WIDTH), f32),
        "gla_norm": gain(ks[12], (DEPTH, GLA_WIDTH)),
        "w_branch": nrm(ks[13], (DEPTH, N_BRANCH, BRANCH_WIDTH, D_MODEL), BRANCH_WIDTH),
        "w_out": nrm(ks[14], (DEPTH, D_MODEL, D_MODEL), D_MODEL),
        "final_norm": gain(ks[15], (D_MODEL,)),
    }


def reference(x, positions, norm_g, w_in, b_merge, mla_q_norm, mla_kv_norm, mla_w_uq, mla_w_ukv,
              pool_w, pool_scale, gla_w_gate, gla_b_gate, gla_norm, w_branch, w_out, final_norm):
    cos, sin = rope_tables(positions)
    h = x
    for l in range(DEPTH):
        h = hybrid_layer(h, cos, sin, norm_g[l], w_in[l], b_merge[l], mla_q_norm[l], mla_kv_norm[l],
                         mla_w_uq[l], mla_w_ukv[l], pool_w[l], pool_scale[l], gla_w_gate[l],
                         gla_b_gate[l], gla_norm[l], w_branch[l], w_out[l])
    return rmsnorm(h, final_norm)
```

```python
import functools
import math

import jax
import jax.numpy as jnp
from jax import lax
from jax.experimental import pallas as pl
from jax.experimental.pallas import tpu as pltpu

f32 = jnp.float32
bf16 = jnp.bfloat16

D_MODEL = 2048
CHUNK = 64
EPS = 1e-6

MLA_HEADS = 8
MLA_NOPE = 128
MLA_ROPE = 64
MLA_V = 128
MLA_LORA = 512
ROPE_THETA = 10000.0
HEAD_PAD = 256

POOL_WINDOWS = (2, 4, 8, 16)
POOL_GROUP_W = 256
POOL_HALO = 16

GLA_HEADS = 4
GLA_DK = 128
GLA_DV = 256
GLA_GATE_RANK = 16
GLA_GATE_TAU = 16.0

N_BRANCH = 3
BRANCH_WIDTH = 1024

_O_CQ, _O_CKV, _O_KR, _O_ZA, _O_UB, _O_ZB = 0, 512, 1024, 1088, 2112, 3136
_O_QC, _O_KC, _O_VC, _O_GLR, _O_ZC, _O_GATES = 4160, 4672, 5184, 6208, 6224, 7248

P_GATES, P_ZA, P_UB, P_ZB, P_VC, P_ZC = 0, 6144, 7168, 8192, 9216, 10240
P_CQ, P_CKV, P_QC, P_KC, P_KR, P_GLR = 11264, 11776, 12288, 12800, 13312, 13440
P_TOTAL = 13824

VMEM_LIMIT = 56 * 1024 * 1024


def _cparams(n_axes):
    return pltpu.CompilerParams(
        dimension_semantics=("arbitrary",) * n_axes, vmem_limit_bytes=VMEM_LIMIT)


def _silu(z):
    return z * (1.0 / (1.0 + jnp.exp(-z)))


def _rms(x, g):
    return x * lax.rsqrt(jnp.mean(x * x, axis=-1, keepdims=True) + EPS) * g


def _rmsnorm_kernel(x_ref, g_ref, o_ref):
    o_ref[...] = _rms(x_ref[...], g_ref[...]).astype(o_ref.dtype)


def _rmsnorm(x, g, tm):
    T, D = x.shape
    return pl.pallas_call(
        _rmsnorm_kernel,
        grid=(T // tm,),
        in_specs=[pl.BlockSpec((tm, D), lambda i: (i, 0)),
                  pl.BlockSpec((1, D), lambda i: (0, 0))],
        out_specs=pl.BlockSpec((tm, D), lambda i: (i, 0)),
        out_shape=jax.ShapeDtypeStruct((T, D), bf16),
        compiler_params=_cparams(1),
        name="rmsnorm",
    )(x, g)


def _inproj_kernel(h_ref, w_ref, o_ref):
    o_ref[...] = jnp.dot(h_ref[...], w_ref[...],
                         preferred_element_type=f32).astype(o_ref.dtype)


def _inproj(h, w, tm, tn):
    T, D = h.shape
    N = w.shape[1]
    return pl.pallas_call(
        _inproj_kernel,
        grid=(T // tm, N // tn),
        in_specs=[pl.BlockSpec((tm, D), lambda i, j: (i, 0)),
                  pl.BlockSpec((D, tn), lambda i, j: (0, j))],
        out_specs=pl.BlockSpec((tm, tn), lambda i, j: (i, j)),
        out_shape=jax.ShapeDtypeStruct((T, N), bf16),
        compiler_params=_cparams(2),
        name="inproj",
    )(h, w)


def _rope_kernel(pos_ref, inv_ref, sign_ref, cos_ref, sin_ref):
    ang = pos_ref[...].astype(f32) * inv_ref[...]
    cos_ref[...] = jnp.cos(ang)
    sin_ref[...] = jnp.sin(ang) * sign_ref[...]


def _rope_tables(pos, tm):
    T = pos.shape[0]
    half = MLA_ROPE // 2
    inv = 1.0 / (ROPE_THETA ** (jnp.arange(0, MLA_ROPE, 2, dtype=f32) / MLA_ROPE))
    zero = jnp.zeros((half,), f32)
    one = jnp.ones((half,), f32)
    inv_row = jnp.concatenate([inv, zero, inv, zero])[None, :]
    sign_row = jnp.concatenate([-one, zero, one, zero])[None, :]
    row = pl.BlockSpec((1, 128), lambda i: (0, 0))
    tab = pl.BlockSpec((tm, 128), lambda i: (i, 0))
    return pl.pallas_call(
        _rope_kernel,
        grid=(T // tm,),
        in_specs=[pl.BlockSpec((tm, 1), lambda i: (i, 0)), row, row],
        out_specs=[tab, tab],
        out_shape=[jax.ShapeDtypeStruct((T, 128), f32)] * 2,
        compiler_params=_cparams(1),
        name="rope_tables",
    )(pos, inv_row, sign_row)


def _mla_up_kernel(cq_ref, ckv_ref, kr_ref, cos_ref, sin_ref, gq_ref, gkv_ref,
                   wuq_ref, wuk_ref, wuvt_ref, q_ref, k_ref, vt_ref, *, qscale):
    cos = cos_ref[...]
    sin = sin_ref[...]

    def rope(x):
        return x * cos + pltpu.roll(x, 64, 1) * sin

    cqn = _rms(cq_ref[...].astype(f32), gq_ref[...]).astype(bf16)
    qf = jnp.dot(cqn, wuq_ref[...], preferred_element_type=f32) * qscale
    for h in range(MLA_HEADS):
        c0 = h * HEAD_PAD
        q_ref[0, h, 0, :, 0:128] = qf[:, c0:c0 + 128].astype(bf16)
        q_ref[0, h, 0, :, 128:256] = rope(qf[:, c0 + 128:c0 + 256]).astype(bf16)

    ckvn = _rms(ckv_ref[...].astype(f32), gkv_ref[...]).astype(bf16)
    kn = jnp.dot(ckvn, wuk_ref[...], preferred_element_type=f32)
    krr = rope(kr_ref[...].astype(f32)).astype(bf16)
    for h in range(MLA_HEADS):
        k_ref[0, h, 0, :, 0:128] = kn[:, h * 128:(h + 1) * 128].astype(bf16)
        k_ref[0, h, 0, :, 128:256] = krr

    vt = lax.dot_general(wuvt_ref[...], ckvn, (((1,), (1,)), ((), ())),
                         preferred_element_type=f32)
    for h in range(MLA_HEADS):
        vt_ref[0, h, 0] = vt[h * 128:(h + 1) * 128, :].astype(bf16)


def _mla_up(proj, cos_t, sin_t, gq, gkv, wuq, wuk, wuvt, B, S, blk):
    nb = S // blk
    H = MLA_HEADS

    def rows(b, j):
        return b * nb + j

    const = lambda b, j: (0, 0)
    qk_shape = jax.ShapeDtypeStruct((B, H, nb, blk, HEAD_PAD), bf16)
    qk_spec = pl.BlockSpec((1, H, 1, blk, HEAD_PAD), lambda b, j: (b, 0, j, 0, 0))
    qscale = (MLA_NOPE + MLA_ROPE) ** -0.5 * math.log2(math.e)
    return pl.pallas_call(
        functools.partial(_mla_up_kernel, qscale=qscale),
        grid=(B, nb),
        in_specs=[
            pl.BlockSpec((blk, 512), lambda b, j: (rows(b, j), P_CQ // 512)),
            pl.BlockSpec((blk, 512), lambda b, j: (rows(b, j), P_CKV // 512)),
            pl.BlockSpec((blk, 128), lambda b, j: (rows(b, j), P_KR // 128)),
            pl.BlockSpec((blk, 128), lambda b, j: (rows(b, j), 0)),
            pl.BlockSpec((blk, 128), lambda b, j: (rows(b, j), 0)),
            pl.BlockSpec((1, MLA_LORA), const),
            pl.BlockSpec((1, MLA_LORA), const),
            pl.BlockSpec(wuq.shape, const),
            pl.BlockSpec(wuk.shape, const),
            pl.BlockSpec(wuvt.shape, const),
        ],
        out_specs=[qk_spec, qk_spec,
                   pl.BlockSpec((1, H, 1, MLA_V, blk), lambda b, j: (b, 0, j, 0, 0))],
        out_shape=[qk_shape, qk_shape,
                   jax.ShapeDtypeStruct((B, H, nb, MLA_V, blk), bf16)],
        compiler_params=_cparams(2),
        name="mla_up",
    )(proj, proj, proj, cos_t, sin_t, gq, gkv, wuq, wuk, wuvt)


def _attn_kernel(q_ref, k_ref, vt_ref, z_ref, o_ref, m_ref, l_ref, acc_ref, *, blk):
    i = pl.program_id(2)
    q = q_ref[0, 0, 0]
    m_ref[...] = jnp.full(m_ref.shape, -1e30, f32)
    l_ref[...] = jnp.zeros(l_ref.shape, f32)
    acc_ref[...] = jnp.zeros(acc_ref.shape, f32)

    def tile(j, masked):
        s = lax.dot_general(k_ref[0, 0, j], q, (((1,), (1,)), ((), ())),
                            preferred_element_type=f32)
        if masked:
            kc = lax.broadcasted_iota(jnp.int32, s.shape, 0) // CHUNK
            qc = lax.broadcasted_iota(jnp.int32, s.shape, 1) // CHUNK
            s = jnp.where(kc <= qc, s, -1e30)
        m_old = m_ref[...]
        m_new = jnp.maximum(m_old, jnp.max(s, axis=0, keepdims=True))
        alpha = jnp.exp2(m_old - m_new)
        p = jnp.exp2(s - m_new)
        l_ref[...] = alpha * l_ref[...] + jnp.sum(p, axis=0, keepdims=True)
        pv = jnp.dot(vt_ref[0, 0, j], p.astype(bf16), preferred_element_type=f32)
        acc_ref[...] = alpha * acc_ref[...] + pv
        m_ref[...] = m_new

    def body(j, carry):
        tile(j, False)
        return carry

    lax.fori_loop(0, i, body, 0)
    tile(i, True)

    o = acc_ref[...] / l_ref[...]
    z = z_ref[...].astype(f32)
    o_ref[...] = (o.T * _silu(z)).astype(o_ref.dtype)


def _attention(q, k, vt, proj, B, S, blk):
    H = MLA_HEADS
    nb = S // blk
    T = B * S
    return pl.pallas_call(
        functools.partial(_attn_kernel, blk=blk),
        grid=(B, H, nb),
        in_specs=[
            pl.BlockSpec((1, 1, 1, blk, HEAD_PAD), lambda b, h, i: (b, h, i, 0, 0)),
            pl.BlockSpec((1, 1, nb, blk, HEAD_PAD), lambda b, h, i: (b, h, 0, 0, 0)),
            pl.BlockSpec((1, 1, nb, MLA_V, blk), lambda b, h, i: (b, h, 0, 0, 0)),
            pl.BlockSpec((blk, MLA_V), lambda b, h, i: (b * nb + i, P_ZA // MLA_V + h)),
        ],
        out_specs=pl.BlockSpec((blk, MLA_V), lambda b, h, i: (b * nb + i, h)),
        out_shape=jax.ShapeDtypeStruct((T, BRANCH_WIDTH), bf16),
        scratch_shapes=[pltpu.VMEM((1, blk), f32), pltpu.VMEM((1, blk), f32),
                        pltpu.VMEM((MLA_V, blk), f32)],
        compiler_params=_cparams(3),
        name="mla_attention",
    )(q, k, vt, proj)


def _pool_kernel(u_ref, z_ref, w_ref, sc_ref, o_ref, ext_ref, *, tm):
    j = pl.program_id(1)
    H = POOL_HALO

    @pl.when(j == 0)
    def _():
        ext_ref[0:H, :] = jnp.zeros((H, ext_ref.shape[1]), f32)

    @pl.when(j > 0)
    def _():
        ext_ref[0:H, :] = ext_ref[tm:tm + H, :]

    u = u_ref[...].astype(f32)
    ext_ref[H:H + tm, :] = u
    t = j * tm + lax.broadcasted_iota(jnp.int32, (tm, 1), 0)
    for g, w in enumerate(POOL_WINDOWS):
        c0, c1 = g * POOL_GROUP_W, (g + 1) * POOL_GROUP_W
        ug = u[:, c0:c1]
        acc = ug
        for kk in range(1, w):
            acc = acc + ext_ref[H - kk:H - kk + tm, c0:c1]
        cnt = jnp.minimum(t + 1, w).astype(f32)
        m = acc / cnt - ug
        y = jnp.dot(m.astype(bf16), w_ref[g], preferred_element_type=f32) * sc_ref[:, c0:c1]
        o_ref[:, c0:c1] = (y * _silu(z_ref[:, c0:c1].astype(f32))).astype(o_ref.dtype)


def _pool(proj, pool_w, pool_scale, B, S, tm):
    nb = S // tm
    T = B * S
    W = BRANCH_WIDTH
    return pl.pallas_call(
        functools.partial(_pool_kernel, tm=tm),
        grid=(B, nb),
        in_specs=[
            pl.BlockSpec((tm, W), lambda b, j: (b * nb + j, P_UB // W)),
            pl.BlockSpec((tm, W), lambda b, j: (b * nb + j, P_ZB // W)),
            pl.BlockSpec(pool_w.shape, lambda b, j: (0, 0, 0)),
            pl.BlockSpec((1, W), lambda b, j: (0, 0)),
        ],
        out_specs=pl.BlockSpec((tm, W), lambda b, j: (b * nb + j, 0)),
        out_shape=jax.ShapeDtypeStruct((T, W), bf16),
        scratch_shapes=[pltpu.VMEM((tm + POOL_HALO, W), f32)],
        compiler_params=_cparams(2),
        name="pool",
    )(proj, proj, pool_w, pool_scale)


def _gla_kernel(q_ref, k_ref, v_ref, glr_ref, z_ref, wg_ref, bg_ref, ng_ref, o_ref,
                st_ref, *, tg):
    j = pl.program_id(1)

    @pl.when(j == 0)
    def _():
        st_ref[...] = jnp.zeros(st_ref.shape, f32)

    x = jnp.dot(glr_ref[...], wg_ref[...], preferred_element_type=f32) + bg_ref[...]
    log_a = -(jnp.maximum(-x, 0.0) + jnp.log1p(jnp.exp(-jnp.abs(x)))) / GLA_GATE_TAU

    r = lax.broadcasted_iota(jnp.int32, (tg, tg), 0)
    c = lax.broadcasted_iota(jnp.int32, (tg, tg), 1)
    tri = ((r // CHUNK == c // CHUNK) & (c <= r)).astype(bf16)
    hi = log_a.astype(bf16)
    rem = log_a - hi.astype(f32)
    mid = rem.astype(bf16)
    lo = (rem - mid.astype(f32)).astype(bf16)
    bcum = (jnp.dot(tri, hi, preferred_element_type=f32)
            + jnp.dot(tri, mid, preferred_element_type=f32)
            + jnp.dot(tri, lo, preferred_element_type=f32))

    q = q_ref[...].astype(f32) * (GLA_DK ** -0.5)
    k = k_ref[...].astype(f32)
    qd = (q * jnp.exp(bcum)).astype(bf16)
    kd = (k * jnp.exp(-bcum)).astype(bf16)
    rr = lax.broadcasted_iota(jnp.int32, (CHUNK, CHUNK), 0)
    cc = lax.broadcasted_iota(jnp.int32, (CHUNK, CHUNK), 1)
    causal = cc <= rr
    nt = (((1,), (1,)), ((), ()))
    for g in range(tg // CHUNK):
        r0, r1 = g * CHUNK, (g + 1) * CHUNK
        bl = bcum[r1 - 1:r1, :]
        ke = (k[r0:r1, :] * jnp.exp(bl - bcum[r0:r1, :])).astype(bf16)
        dec = jnp.exp(bl)
        for h in range(GLA_HEADS):
            k0, k1 = h * GLA_DK, (h + 1) * GLA_DK
            v0, v1 = h * GLA_DV, (h + 1) * GLA_DV
            v = v_ref[r0:r1, v0:v1]
            qh = qd[r0:r1, k0:k1]
            att = lax.dot_general(qh, kd[r0:r1, k0:k1], nt, preferred_element_type=f32)
            att = jnp.where(causal, att, 0.0).astype(bf16)
            st = st_ref[h]
            o = (jnp.dot(att, v, preferred_element_type=f32)
                 + lax.dot_general(qh, st.astype(bf16), nt, preferred_element_type=f32))
            ut = lax.dot_general(v, ke[:, k0:k1], (((0,), (0,)), ((), ())),
                                 preferred_element_type=f32)
            st_ref[h] = st * dec[:, k0:k1] + ut
            on = _rms(o, ng_ref[:, v0:v1])
            o_ref[r0:r1, v0:v1] = (on * _silu(z_ref[r0:r1, v0:v1].astype(f32))).astype(o_ref.dtype)


def _gla(proj, wg, bg, ng, B, S, tg):
    nb = S // tg
    T = B * S
    W = BRANCH_WIDTH
    QK = GLA_HEADS * GLA_DK
    rows = lambda b, j: b * nb + j
    return pl.pallas_call(
        functools.partial(_gla_kernel, tg=tg),
        grid=(B, nb),
        in_specs=[
            pl.BlockSpec((tg, QK), lambda b, j: (rows(b, j), P_QC // QK)),
            pl.BlockSpec((tg, QK), lambda b, j: (rows(b, j), P_KC // QK)),
            pl.BlockSpec((tg, W), lambda b, j: (rows(b, j), P_VC // W)),
            pl.BlockSpec((tg, 128), lambda b, j: (rows(b, j), P_GLR // 128)),
            pl.BlockSpec((tg, W), lambda b, j: (rows(b, j), P_ZC // W)),
            pl.BlockSpec(wg.shape, lambda b, j: (0, 0)),
            pl.BlockSpec((1, QK), lambda b, j: (0, 0)),
            pl.BlockSpec((1, W), lambda b, j: (0, 0)),
        ],
        out_specs=pl.BlockSpec((tg, W), lambda b, j: (rows(b, j), 0)),
        out_shape=jax.ShapeDtypeStruct((T, W), bf16),
        scratch_shapes=[pltpu.VMEM((GLA_HEADS, GLA_DV, GLA_DK), f32)],
        compiler_params=_cparams(2),
        name="gla",
    )(proj, proj, proj, proj, proj, wg, bg, ng)


def _merge_kernel(ya_ref, yb_ref, yc_ref, g0_ref, g1_ref, g2_ref, bm_ref, wb_ref, wo_ref,
                  x_ref, ng_ref, *out_refs, final):
    merged = None
    for n, (y_ref, g_ref) in enumerate(((ya_ref, g0_ref), (yb_ref, g1_ref), (yc_ref, g2_ref))):
        p = jnp.dot(y_ref[...], wb_ref[n], preferred_element_type=f32)
        gate = 1.0 / (1.0 + jnp.exp(-(g_ref[...].astype(f32) + bm_ref[n:n + 1, :])))
        merged = gate * p if merged is None else merged + gate * p
    xn = x_ref[...] + jnp.dot(merged.astype(bf16), wo_ref[...], preferred_element_type=f32)
    hn = _rms(xn, ng_ref[...])
    if final:
        out_refs[0][...] = hn
    else:
        out_refs[0][...] = xn
        out_refs[1][...] = hn.astype(bf16)


def _merge(ya, yb, yc, proj, bm, wb, wo, x, ng, tm, final):
    T, D = x.shape
    W = BRANCH_WIDTH
    row = lambda i: (i, 0)
    ytile = pl.BlockSpec((tm, W), row)
    xtile = pl.BlockSpec((tm, D), row)
    resident = dict(pipeline_mode=pl.Buffered(1))
    if final:
        out_specs = [xtile]
        out_shape = [jax.ShapeDtypeStruct((T, D), f32)]
    else:
        out_specs = [xtile, xtile]
        out_shape = [jax.ShapeDtypeStruct((T, D), f32), jax.ShapeDtypeStruct((T, D), bf16)]
    return pl.pallas_call(
        functools.partial(_merge_kernel, final=final),
        grid=(T // tm,),
        in_specs=[
            ytile, ytile, ytile,
            pl.BlockSpec((tm, D), lambda i: (i, P_GATES // D + 0)),
            pl.BlockSpec((tm, D), lambda i: (i, P_GATES // D + 1)),
            pl.BlockSpec((tm, D), lambda i: (i, P_GATES // D + 2)),
            pl.BlockSpec((N_BRANCH, D), lambda i: (0, 0)),
            pl.BlockSpec(wb.shape, lambda i: (0, 0, 0), **resident),
            pl.BlockSpec(wo.shape, lambda i: (0, 0), **resident),
            xtile,
            pl.BlockSpec((1, D), lambda i: (0, 0)),
        ],
        out_specs=out_specs,
        out_shape=out_shape,
        compiler_params=_cparams(1),
        name="merge_final" if final else "merge",
    )(ya, yb, yc, proj, proj, proj, bm, wb, wo, x, ng)


def _permute_w_in(w):
    D = w.shape[0]
    half = MLA_ROPE // 2
    sl = lambda o, n: w[:, o:o + n]
    z = lambda n: jnp.zeros((D, n), w.dtype)
    parts = [
        sl(_O_GATES, N_BRANCH * D_MODEL), sl(_O_ZA, 1024), sl(_O_UB, 1024), sl(_O_ZB, 1024),
        sl(_O_VC, 1024), sl(_O_ZC, 1024), sl(_O_CQ, 512), sl(_O_CKV, 512),
        sl(_O_QC, 512), sl(_O_KC, 512),
        sl(_O_KR, half), z(half), sl(_O_KR + half, half), z(half),
        sl(_O_GLR, GLA_GATE_RANK), z(128 - GLA_GATE_RANK),
    ]
    used = P_GLR + 128
    parts.append(z(P_TOTAL - used))
    return jnp.concatenate(parts, axis=1).astype(bf16)


def _permute_w_uq(w):
    half = MLA_ROPE // 2
    w = w.reshape(MLA_LORA, MLA_HEADS, MLA_NOPE + MLA_ROPE)
    z = jnp.zeros((MLA_LORA, MLA_HEADS, half), w.dtype)
    out = jnp.concatenate([w[..., :MLA_NOPE], w[..., MLA_NOPE:MLA_NOPE + half], z,
                           w[..., MLA_NOPE + half:], z], axis=-1)
    return out.reshape(MLA_LORA, MLA_HEADS * HEAD_PAD).astype(bf16)


def _split_w_ukv(w):
    w = w.reshape(MLA_LORA, MLA_HEADS, MLA_NOPE + MLA_V)
    wuk = w[..., :MLA_NOPE].reshape(MLA_LORA, MLA_HEADS * MLA_NOPE).astype(bf16)
    wuvt = w[..., MLA_NOPE:].reshape(MLA_LORA, MLA_HEADS * MLA_V).T.astype(bf16)
    return wuk, wuvt


def _pick(n, pref):
    return pref if n % pref == 0 else n


def kernel(x, positions, norm_g, w_in, b_merge, mla_q_norm, mla_kv_norm, mla_w_uq, mla_w_ukv,
           pool_w, pool_scale, gla_w_gate, gla_b_gate, gla_norm, w_branch, w_out, final_norm):
    B, S, D = x.shape
    assert D == D_MODEL and S % 512 == 0
    T = B * S
    depth = norm_g.shape[0]
    blk = 512
    tm_proj = _pick(T, 1024)
    tn_proj = 1536
    tm_pool = 512
    tg = 256
    tm_merge = 256

    xf = x.reshape(T, D)
    cos_t, sin_t = _rope_tables(positions.reshape(T, 1), 512)
    h = _rmsnorm(xf, norm_g[0][None, :], 512)
    out = None
    for l in range(depth):
        w_in_p = _permute_w_in(w_in[l])
        wuq = _permute_w_uq(mla_w_uq[l])
        wuk, wuvt = _split_w_ukv(mla_w_ukv[l])
        wg = jnp.zeros((128, GLA_HEADS * GLA_DK), bf16).at[:GLA_GATE_RANK].set(
            gla_w_gate[l].astype(bf16))

        proj = _inproj(h, w_in_p, tm_proj, tn_proj)
        q, k, vt = _mla_up(proj, cos_t, sin_t, mla_q_norm[l][None, :], mla_kv_norm[l][None, :],
                           wuq, wuk, wuvt, B, S, blk)
        ya = _attention(q, k, vt, proj, B, S, blk)
        yb = _pool(proj, pool_w[l].astype(bf16), pool_scale[l][None, :], B, S, tm_pool)
        yc = _gla(proj, wg, gla_b_gate[l][None, :], gla_norm[l][None, :], B, S, tg)

        final = l == depth - 1
        next_g = final_norm if final else norm_g[l + 1]
        res = _merge(ya, yb, yc, proj, b_merge[l], w_branch[l].astype(bf16),
                     w_out[l].astype(bf16), xf, next_g[None, :], tm_merge, final)
        if final:
            out = res[0]
        else:
            xf, h = res
    return out.reshape(B, S, D)
```

```python
import functools
import math

import jax
import jax.numpy as jnp
from jax import lax
from jax.experimental import pallas as pl
from jax.experimental.pallas import tpu as pltpu

f32 = jnp.float32
bf16 = jnp.bfloat16

D_MODEL = 2048
CHUNK = 64
EPS = 1e-6

MLA_HEADS = 8
MLA_NOPE = 128
MLA_ROPE = 64
MLA_V = 128
VT_ROWS = MLA_V + 16
MLA_LORA = 512
ROPE_THETA = 10000.0
HEAD_PAD = 256

POOL_WINDOWS = (2, 4, 8, 16)
POOL_GROUP_W = 256
POOL_HALO = 16

GLA_HEADS = 4
GLA_DK = 128
GLA_DV = 256
GLA_GATE_RANK = 16
GLA_GATE_TAU = 16.0

N_BRANCH = 3
BRANCH_WIDTH = 1024

_O_CQ, _O_CKV, _O_KR, _O_ZA, _O_UB, _O_ZB = 0, 512, 1024, 1088, 2112, 3136
_O_QC, _O_KC, _O_VC, _O_GLR, _O_ZC, _O_GATES = 4160, 4672, 5184, 6208, 6224, 7248

P_GATES, P_ZA, P_UB, P_ZB, P_VC, P_ZC = 0, 6144, 7168, 8192, 9216, 10240
P_CQ, P_CKV, P_QC, P_KC, P_KR, P_GLR = 11264, 11776, 12288, 12800, 13312, 13440
P_TOTAL = 13824

VMEM_LIMIT = 56 * 1024 * 1024


def _cparams(n_axes, flags=None):
    return pltpu.CompilerParams(
        dimension_semantics=("arbitrary",) * n_axes, vmem_limit_bytes=VMEM_LIMIT, flags=flags)


def _silu(z):
    return z * (1.0 / (1.0 + jnp.exp(-z)))


def _rms(x, g):
    return x * lax.rsqrt(jnp.mean(x * x, axis=-1, keepdims=True) + EPS) * g


def _rmsnorm_kernel(x_ref, g_ref, o_ref):
    o_ref[...] = _rms(x_ref[...], g_ref[...]).astype(o_ref.dtype)


def _rmsnorm(x, g, tm):
    T, D = x.shape
    return pl.pallas_call(
        _rmsnorm_kernel,
        grid=(T // tm,),
        in_specs=[pl.BlockSpec((tm, D), lambda i: (i, 0)),
                  pl.BlockSpec((1, D), lambda i: (0, 0))],
        out_specs=pl.BlockSpec((tm, D), lambda i: (i, 0)),
        out_shape=jax.ShapeDtypeStruct((T, D), bf16),
        compiler_params=_cparams(1),
        name="rmsnorm",
    )(x, g)


def _inproj_kernel(h_ref, w_ref, o_ref):
    o_ref[...] = jnp.dot(h_ref[...], w_ref[...],
                         preferred_element_type=f32).astype(o_ref.dtype)


def _inproj(h, w, tm, tn):
    T, D = h.shape
    N = w.shape[1]
    return pl.pallas_call(
        _inproj_kernel,
        grid=(T // tm, N // tn),
        in_specs=[pl.BlockSpec((tm, D), lambda i, j: (i, 0)),
                  pl.BlockSpec((D, tn), lambda i, j: (0, j))],
        out_specs=pl.BlockSpec((tm, tn), lambda i, j: (i, j)),
        out_shape=jax.ShapeDtypeStruct((T, N), bf16),
        compiler_params=_cparams(2),
        name="inproj",
    )(h, w)


def _rope_kernel(pos_ref, inv_ref, sign_ref, cos_ref, sin_ref):
    ang = pos_ref[...].astype(f32) * inv_ref[...]
    cos_ref[...] = jnp.cos(ang)
    sin_ref[...] = jnp.sin(ang) * sign_ref[...]


def _rope_tables(pos, tm):
    T = pos.shape[0]
    half = MLA_ROPE // 2
    inv = 1.0 / (ROPE_THETA ** (jnp.arange(0, MLA_ROPE, 2, dtype=f32) / MLA_ROPE))
    zero = jnp.zeros((half,), f32)
    one = jnp.ones((half,), f32)
    inv_row = jnp.concatenate([inv, zero, inv, zero])[None, :]
    sign_row = jnp.concatenate([-one, zero, one, zero])[None, :]
    row = pl.BlockSpec((1, 128), lambda i: (0, 0))
    tab = pl.BlockSpec((tm, 128), lambda i: (i, 0))
    return pl.pallas_call(
        _rope_kernel,
        grid=(T // tm,),
        in_specs=[pl.BlockSpec((tm, 1), lambda i: (i, 0)), row, row],
        out_specs=[tab, tab],
        out_shape=[jax.ShapeDtypeStruct((T, 128), f32)] * 2,
        compiler_params=_cparams(1),
        name="rope_tables",
    )(pos, inv_row, sign_row)


def _mla_up_kernel(cq_ref, ckv_ref, kr_ref, cos_ref, sin_ref, gq_ref, gkv_ref,
                   wuq_ref, wuk_ref, wuvt_ref, q_ref, k_ref, vt_ref, *, qscale):
    cos = cos_ref[...]
    sin = sin_ref[...]

    def rope(x):
        return x * cos + pltpu.roll(x, 64, 1) * sin

    cqn = _rms(cq_ref[...].astype(f32), gq_ref[...]).astype(bf16)
    qf = jnp.dot(cqn, wuq_ref[...], preferred_element_type=f32) * qscale
    for h in range(MLA_HEADS):
        c0 = h * HEAD_PAD
        q_ref[0, h, 0, :, 0:128] = qf[:, c0:c0 + 128].astype(bf16)
        q_ref[0, h, 0, :, 128:256] = rope(qf[:, c0 + 128:c0 + 256]).astype(bf16)

    ckvn = _rms(ckv_ref[...].astype(f32), gkv_ref[...]).astype(bf16)
    kn = jnp.dot(ckvn, wuk_ref[...], preferred_element_type=f32)
    krr = rope(kr_ref[...].astype(f32)).astype(bf16)
    for h in range(MLA_HEADS):
        k_ref[0, h, 0, :, 0:128] = kn[:, h * 128:(h + 1) * 128].astype(bf16)
        k_ref[0, h, 0, :, 128:256] = krr

    vt = lax.dot_general(wuvt_ref[...], ckvn, (((1,), (1,)), ((), ())),
                         preferred_element_type=f32)
    ones = jnp.ones((VT_ROWS - MLA_V, vt.shape[1]), bf16)
    for h in range(MLA_HEADS):
        vt_ref[0, h, 0, 0:MLA_V, :] = vt[h * MLA_V:(h + 1) * MLA_V, :].astype(bf16)
        vt_ref[0, h, 0, MLA_V:VT_ROWS, :] = ones


def _mla_up(proj, cos_t, sin_t, gq, gkv, wuq, wuk, wuvt, B, S, blk):
    nb = S // blk
    H = MLA_HEADS

    def rows(b, j):
        return b * nb + j

    const = lambda b, j: (0, 0)
    qk_shape = jax.ShapeDtypeStruct((B, H, nb, blk, HEAD_PAD), bf16)
    qk_spec = pl.BlockSpec((1, H, 1, blk, HEAD_PAD), lambda b, j: (b, 0, j, 0, 0))
    qscale = (MLA_NOPE + MLA_ROPE) ** -0.5 * math.log2(math.e)
    return pl.pallas_call(
        functools.partial(_mla_up_kernel, qscale=qscale),
        grid=(B, nb),
        in_specs=[
            pl.BlockSpec((blk, 512), lambda b, j: (rows(b, j), P_CQ // 512)),
            pl.BlockSpec((blk, 512), lambda b, j: (rows(b, j), P_CKV // 512)),
            pl.BlockSpec((blk, 128), lambda b, j: (rows(b, j), P_KR // 128)),
            pl.BlockSpec((blk, 128), lambda b, j: (rows(b, j), 0)),
            pl.BlockSpec((blk, 128), lambda b, j: (rows(b, j), 0)),
            pl.BlockSpec((1, MLA_LORA), const),
            pl.BlockSpec((1, MLA_LORA), const),
            pl.BlockSpec(wuq.shape, const),
            pl.BlockSpec(wuk.shape, const),
            pl.BlockSpec(wuvt.shape, const),
        ],
        out_specs=[qk_spec, qk_spec,
                   pl.BlockSpec((1, H, 1, VT_ROWS, blk), lambda b, j: (b, 0, j, 0, 0))],
        out_shape=[qk_shape, qk_shape,
                   jax.ShapeDtypeStruct((B, H, nb, VT_ROWS, blk), bf16)],
        compiler_params=_cparams(2),
        name="mla_up",
    )(proj, proj, proj, cos_t, sin_t, gq, gkv, wuq, wuk, wuvt)


def _attn_kernel(q_ref, k_ref, vt_ref, z_ref, o_ref,
                 m_ref, al0_ref, al1_ref, mt0_ref, mt1_ref, acc_ref, s0_ref, s1_ref, *, tq, tk):
    i = pl.program_id(2)
    kpq = tq // tk
    assert kpq % 2 == 0
    n_full = i * kpq
    nt = (((1,), (1,)), ((), ()))
    m_ref[...] = jnp.full(m_ref.shape, -1e30, f32)
    acc_ref[...] = jnp.zeros(acc_ref.shape, f32)

    def scores(j, cols):
        return lax.dot_general(k_ref[0, 0, j], q_ref[0, 0, 0, cols, :], nt,
                               preferred_element_type=f32)

    def put_scores(s, bufs, cols):
        s_ref, mt_ref, _ = bufs
        s_ref[:, cols] = s
        mt_ref[:, cols] = jnp.max(s, axis=0, keepdims=True)

    def softmax_pv(j, bufs, cols):
        s_ref, mt_ref, al_ref = bufs
        m_old = m_ref[:, cols]
        m_new = jnp.maximum(m_old, mt_ref[:, cols])
        al_ref[:, cols] = jnp.exp2(m_old - m_new)
        m_ref[:, cols] = m_new
        p = jnp.exp2(s_ref[:, cols] - m_new).astype(bf16)
        acc_ref[:, cols] = (al_ref[:, cols] * acc_ref[:, cols]
                            + jnp.dot(vt_ref[0, 0, j], p, preferred_element_type=f32))

    everything = slice(0, tq)
    buf0, buf1 = (s0_ref, mt0_ref, al0_ref), (s1_ref, mt1_ref, al1_ref)
    put_scores(scores(0, everything), buf0, everything)

    def body(t, carry):
        j0 = 2 * t
        put_scores(scores(j0 + 1, everything), buf1, everything)
        softmax_pv(j0, buf0, everything)
        put_scores(scores(j0 + 2, everything), buf0, everything)
        softmax_pv(j0 + 1, buf1, everything)
        return carry

    lax.fori_loop(0, n_full // 2, body, 0)

    for kt in range(kpq):
        cols = slice(kt * tk, tq)
        bufs = buf0 if kt % 2 == 0 else buf1
        s = bufs[0][:, cols] if kt == 0 else scores(n_full + kt, cols)
        kc = lax.broadcasted_iota(jnp.int32, s.shape, 0) // CHUNK
        qc = lax.broadcasted_iota(jnp.int32, s.shape, 1) // CHUNK
        put_scores(jnp.where(kc <= qc, s, -1e30), bufs, cols)
        softmax_pv(n_full + kt, bufs, cols)

    o = acc_ref[0:MLA_V, :] / acc_ref[MLA_V:MLA_V + 1, :]
    z = z_ref[...].astype(f32)
    o_ref[...] = (o.T * _silu(z)).astype(o_ref.dtype)


def _attention(q, k, vt, proj, B, S, tq, tk):
    H = MLA_HEADS
    nq, nk = S // tq, S // tk
    T = B * S
    q = q.reshape(B, H, nq, tq, HEAD_PAD)
    row = pltpu.VMEM((1, tq), f32)
    return pl.pallas_call(
        functools.partial(_attn_kernel, tq=tq, tk=tk),
        grid=(B, H, nq),
        in_specs=[
            pl.BlockSpec((1, 1, 1, tq, HEAD_PAD), lambda b, h, i: (b, h, i, 0, 0)),
            pl.BlockSpec((1, 1, nk, tk, HEAD_PAD), lambda b, h, i: (b, h, 0, 0, 0)),
            pl.BlockSpec((1, 1, nk, VT_ROWS, tk), lambda b, h, i: (b, h, 0, 0, 0)),
            pl.BlockSpec((tq, MLA_V), lambda b, h, i: (b * nq + i, P_ZA // MLA_V + h)),
        ],
        out_specs=pl.BlockSpec((tq, MLA_V), lambda b, h, i: (b * nq + i, h)),
        out_shape=jax.ShapeDtypeStruct((T, BRANCH_WIDTH), bf16),
        scratch_shapes=[row, row, row, row, row, pltpu.VMEM((VT_ROWS, tq), f32),
                        pltpu.VMEM((tk, tq), f32), pltpu.VMEM((tk, tq), f32)],
        compiler_params=_cparams(3),
        name="mla_attention",
    )(q, k, vt, proj)


def _pool_kernel(u_ref, z_ref, w_ref, sc_ref, o_ref, ext_ref, *, tm):
    j = pl.program_id(1)
    H = POOL_HALO

    @pl.when(j == 0)
    def _():
        ext_ref[0:H, :] = jnp.zeros((H, ext_ref.shape[1]), f32)

    @pl.when(j > 0)
    def _():
        ext_ref[0:H, :] = ext_ref[tm:tm + H, :]

    u = u_ref[...].astype(f32)
    ext_ref[H:H + tm, :] = u
    t = j * tm + lax.broadcasted_iota(jnp.int32, (tm, 1), 0)
    for g, w in enumerate(POOL_WINDOWS):
        c0, c1 = g * POOL_GROUP_W, (g + 1) * POOL_GROUP_W
        ug = u[:, c0:c1]
        acc = ug
        for kk in range(1, w):
            acc = acc + ext_ref[H - kk:H - kk + tm, c0:c1]
        cnt = jnp.minimum(t + 1, w).astype(f32)
        m = acc / cnt - ug
        y = jnp.dot(m.astype(bf16), w_ref[g], preferred_element_type=f32) * sc_ref[:, c0:c1]
        o_ref[:, c0:c1] = (y * _silu(z_ref[:, c0:c1].astype(f32))).astype(o_ref.dtype)


def _pool(proj, pool_w, pool_scale, B, S, tm):
    nb = S // tm
    T = B * S
    W = BRANCH_WIDTH
    return pl.pallas_call(
        functools.partial(_pool_kernel, tm=tm),
        grid=(B, nb),
        in_specs=[
            pl.BlockSpec((tm, W), lambda b, j: (b * nb + j, P_UB // W)),
            pl.BlockSpec((tm, W), lambda b, j: (b * nb + j, P_ZB // W)),
            pl.BlockSpec(pool_w.shape, lambda b, j: (0, 0, 0)),
            pl.BlockSpec((1, W), lambda b, j: (0, 0)),
        ],
        out_specs=pl.BlockSpec((tm, W), lambda b, j: (b * nb + j, 0)),
        out_shape=jax.ShapeDtypeStruct((T, W), bf16),
        scratch_shapes=[pltpu.VMEM((tm + POOL_HALO, W), f32)],
        compiler_params=_cparams(2),
        name="pool",
    )(proj, proj, pool_w, pool_scale)


def _gla_kernel(q_ref, k_ref, v_ref, glr_ref, z_ref, wg_ref, bg_ref, ng_ref, o_ref,
                st_ref, *, tg):
    j = pl.program_id(1)

    @pl.when(j == 0)
    def _():
        st_ref[...] = jnp.zeros(st_ref.shape, f32)

    x = jnp.dot(glr_ref[...], wg_ref[...], preferred_element_type=f32) + bg_ref[...]
    log_a = -(jnp.maximum(-x, 0.0) + jnp.log1p(jnp.exp(-jnp.abs(x)))) / GLA_GATE_TAU

    r = lax.broadcasted_iota(jnp.int32, (tg, tg), 0)
    c = lax.broadcasted_iota(jnp.int32, (tg, tg), 1)
    tri = ((r // CHUNK == c // CHUNK) & (c <= r)).astype(bf16)
    hi = log_a.astype(bf16)
    rem = log_a - hi.astype(f32)
    mid = rem.astype(bf16)
    lo = (rem - mid.astype(f32)).astype(bf16)
    bcum = (jnp.dot(tri, hi, preferred_element_type=f32)
            + jnp.dot(tri, mid, preferred_element_type=f32)
            + jnp.dot(tri, lo, preferred_element_type=f32))

    q = q_ref[...].astype(f32) * (GLA_DK ** -0.5)
    k = k_ref[...].astype(f32)
    qd = (q * jnp.exp(bcum)).astype(bf16)
    kd = (k * jnp.exp(-bcum)).astype(bf16)
    rr = lax.broadcasted_iota(jnp.int32, (CHUNK, CHUNK), 0)
    cc = lax.broadcasted_iota(jnp.int32, (CHUNK, CHUNK), 1)
    causal = cc <= rr
    nt = (((1,), (1,)), ((), ()))
    for g in range(tg // CHUNK):
        r0, r1 = g * CHUNK, (g + 1) * CHUNK
        bl = bcum[r1 - 1:r1, :]
        ke = (k[r0:r1, :] * jnp.exp(bl - bcum[r0:r1, :])).astype(bf16)
        dec = jnp.exp(bl)
        for h in range(GLA_HEADS):
            k0, k1 = h * GLA_DK, (h + 1) * GLA_DK
            v0, v1 = h * GLA_DV, (h + 1) * GLA_DV
            v = v_ref[r0:r1, v0:v1]
            qh = qd[r0:r1, k0:k1]
            att = lax.dot_general(qh, kd[r0:r1, k0:k1], nt, preferred_element_type=f32)
            att = jnp.where(causal, att, 0.0).astype(bf16)
            st = st_ref[h]
            o = (jnp.dot(att, v, preferred_element_type=f32)
                 + lax.dot_general(qh, st.astype(bf16), nt, preferred_element_type=f32))
            ut = lax.dot_general(v, ke[:, k0:k1], (((0,), (0,)), ((), ())),
                                 preferred_element_type=f32)
            st_ref[h] = st * dec[:, k0:k1] + ut
            on = _rms(o, ng_ref[:, v0:v1])
            o_ref[r0:r1, v0:v1] = (on * _silu(z_ref[r0:r1, v0:v1].astype(f32))).astype(o_ref.dtype)


def _gla(proj, wg, bg, ng, B, S, tg):
    nb = S // tg
    T = B * S
    W = BRANCH_WIDTH
    QK = GLA_HEADS * GLA_DK
    rows = lambda b, j: b * nb + j
    return pl.pallas_call(
        functools.partial(_gla_kernel, tg=tg),
        grid=(B, nb),
        in_specs=[
            pl.BlockSpec((tg, QK), lambda b, j: (rows(b, j), P_QC // QK)),
            pl.BlockSpec((tg, QK), lambda b, j: (rows(b, j), P_KC // QK)),
            pl.BlockSpec((tg, W), lambda b, j: (rows(b, j), P_VC // W)),
            pl.BlockSpec((tg, 128), lambda b, j: (rows(b, j), P_GLR // 128)),
            pl.BlockSpec((tg, W), lambda b, j: (rows(b, j), P_ZC // W)),
            pl.BlockSpec(wg.shape, lambda b, j: (0, 0)),
            pl.BlockSpec((1, QK), lambda b, j: (0, 0)),
            pl.BlockSpec((1, W), lambda b, j: (0, 0)),
        ],
        out_specs=pl.BlockSpec((tg, W), lambda b, j: (rows(b, j), 0)),
        out_shape=jax.ShapeDtypeStruct((T, W), bf16),
        scratch_shapes=[pltpu.VMEM((GLA_HEADS, GLA_DV, GLA_DK), f32)],
        compiler_params=_cparams(2),
        name="gla",
    )(proj, proj, proj, proj, proj, wg, bg, ng)


def _merge_kernel(ya_ref, yb_ref, yc_ref, g0_ref, g1_ref, g2_ref, bm_ref, wb_ref, wo_ref,
                  x_ref, ng_ref, *out_refs, final):
    merged = None
    for n, (y_ref, g_ref) in enumerate(((ya_ref, g0_ref), (yb_ref, g1_ref), (yc_ref, g2_ref))):
        p = jnp.dot(y_ref[...], wb_ref[n], preferred_element_type=f32)
        gate = 1.0 / (1.0 + jnp.exp(-(g_ref[...].astype(f32) + bm_ref[n:n + 1, :])))
        merged = gate * p if merged is None else merged + gate * p
    xn = x_ref[...] + jnp.dot(merged.astype(bf16), wo_ref[...], preferred_element_type=f32)
    hn = _rms(xn, ng_ref[...])
    if final:
        out_refs[0][...] = hn
    else:
        out_refs[0][...] = xn
        out_refs[1][...] = hn.astype(bf16)


def _merge(ya, yb, yc, proj, bm, wb, wo, x, ng, tm, final):
    T, D = x.shape
    W = BRANCH_WIDTH
    row = lambda i: (i, 0)
    ytile = pl.BlockSpec((tm, W), row)
    xtile = pl.BlockSpec((tm, D), row)
    resident = dict(pipeline_mode=pl.Buffered(1))
    if final:
        out_specs = [xtile]
        out_shape = [jax.ShapeDtypeStruct((T, D), f32)]
    else:
        out_specs = [xtile, xtile]
        out_shape = [jax.ShapeDtypeStruct((T, D), f32), jax.ShapeDtypeStruct((T, D), bf16)]
    return pl.pallas_call(
        functools.partial(_merge_kernel, final=final),
        grid=(T // tm,),
        in_specs=[
            ytile, ytile, ytile,
            pl.BlockSpec((tm, D), lambda i: (i, P_GATES // D + 0)),
            pl.BlockSpec((tm, D), lambda i: (i, P_GATES // D + 1)),
            pl.BlockSpec((tm, D), lambda i: (i, P_GATES // D + 2)),
            pl.BlockSpec((N_BRANCH, D), lambda i: (0, 0)),
            pl.BlockSpec(wb.shape, lambda i: (0, 0, 0), **resident),
            pl.BlockSpec(wo.shape, lambda i: (0, 0), **resident),
            xtile,
            pl.BlockSpec((1, D), lambda i: (0, 0)),
        ],
        out_specs=out_specs,
        out_shape=out_shape,
        compiler_params=_cparams(1),
        name="merge_final" if final else "merge",
    )(ya, yb, yc, proj, proj, proj, bm, wb, wo, x, ng)


def _permute_w_in(w):
    D = w.shape[0]
    half = MLA_ROPE // 2
    sl = lambda o, n: w[:, o:o + n]
    z = lambda n: jnp.zeros((D, n), w.dtype)
    parts = [
        sl(_O_GATES, N_BRANCH * D_MODEL), sl(_O_ZA, 1024), sl(_O_UB, 1024), sl(_O_ZB, 1024),
        sl(_O_VC, 1024), sl(_O_ZC, 1024), sl(_O_CQ, 512), sl(_O_CKV, 512),
        sl(_O_QC, 512), sl(_O_KC, 512),
        sl(_O_KR, half), z(half), sl(_O_KR + half, half), z(half),
        sl(_O_GLR, GLA_GATE_RANK), z(128 - GLA_GATE_RANK),
    ]
    used = P_GLR + 128
    parts.append(z(P_TOTAL - used))
    return jnp.concatenate(parts, axis=1).astype(bf16)


def _permute_w_uq(w):
    half = MLA_ROPE // 2
    w = w.reshape(MLA_LORA, MLA_HEADS, MLA_NOPE + MLA_ROPE)
    z = jnp.zeros((MLA_LORA, MLA_HEADS, half), w.dtype)
    out = jnp.concatenate([w[..., :MLA_NOPE], w[..., MLA_NOPE:MLA_NOPE + half], z,
                           w[..., MLA_NOPE + half:], z], axis=-1)
    return out.reshape(MLA_LORA, MLA_HEADS * HEAD_PAD).astype(bf16)


def _split_w_ukv(w):
    w = w.reshape(MLA_LORA, MLA_HEADS, MLA_NOPE + MLA_V)
    wuk = w[..., :MLA_NOPE].reshape(MLA_LORA, MLA_HEADS * MLA_NOPE).astype(bf16)
    wuvt = w[..., MLA_NOPE:].reshape(MLA_LORA, MLA_HEADS * MLA_V).T.astype(bf16)
    return wuk, wuvt


def _pick(n, pref):
    return pref if n % pref == 0 else n


def kernel(x, positions, norm_g, w_in, b_merge, mla_q_norm, mla_kv_norm, mla_w_uq, mla_w_ukv,
           pool_w, pool_scale, gla_w_gate, gla_b_gate, gla_norm, w_branch, w_out, final_norm):
    B, S, D = x.shape
    assert D == D_MODEL and S % 1024 == 0
    T = B * S
    depth = norm_g.shape[0]
    blk = 512
    tq_attn = 1024
    tm_proj = _pick(T, 1024)
    tn_proj = 1536
    tm_pool = 512
    tg = 256
    tm_merge = 256

    xf = x.reshape(T, D)
    cos_t, sin_t = _rope_tables(positions.reshape(T, 1), 512)
    h = _rmsnorm(xf, norm_g[0][None, :], 512)
    out = None
    for l in range(depth):
        w_in_p = _permute_w_in(w_in[l])
        wuq = _permute_w_uq(mla_w_uq[l])
        wuk, wuvt = _split_w_ukv(mla_w_ukv[l])
        wg = jnp.zeros((128, GLA_HEADS * GLA_DK), bf16).at[:GLA_GATE_RANK].set(
            gla_w_gate[l].astype(bf16))

        proj = _inproj(h, w_in_p, tm_proj, tn_proj)
        q, k, vt = _mla_up(proj, cos_t, sin_t, mla_q_norm[l][None, :], mla_kv_norm[l][None, :],
                           wuq, wuk, wuvt, B, S, blk)
        ya = _attention(q, k, vt, proj, B, S, tq_attn, blk)
        yb = _pool(proj, pool_w[l].astype(bf16), pool_scale[l][None, :], B, S, tm_pool)
        yc = _gla(proj, wg, gla_b_gate[l][None, :], gla_norm[l][None, :], B, S, tg)

        final = l == depth - 1
        next_g = final_norm if final else norm_g[l + 1]
        res = _merge(ya, yb, yc, proj, b_merge[l], w_branch[l].astype(bf16),
                     w_out[l].astype(bf16), xf, next_g[None, :], tm_merge, final)
        if final:
            out = res[0]
        else:
            xf, h = res
    return out.reshape(B, S, D)
```

```python
import functools
import math

import jax
import jax.numpy as jnp
from jax import lax
from jax.experimental import pallas as pl
from jax.experimental.pallas import tpu as pltpu

f32 = jnp.float32
bf16 = jnp.bfloat16

D_MODEL = 2048
CHUNK = 64
EPS = 1e-6

MLA_HEADS = 8
MLA_NOPE = 128
MLA_ROPE = 64
MLA_V = 128
VT_ROWS = MLA_V + 16
MLA_LORA = 512
ROPE_THETA = 10000.0
HEAD_PAD = 256

POOL_WINDOWS = (2, 4, 8, 16)
POOL_GROUP_W = 256
POOL_HALO = 16

GLA_HEADS = 4
GLA_DK = 128
GLA_DV = 256
GLA_GATE_RANK = 16
GLA_GATE_TAU = 16.0

N_BRANCH = 3
BRANCH_WIDTH = 1024

_O_CQ, _O_CKV, _O_KR, _O_ZA, _O_UB, _O_ZB = 0, 512, 1024, 1088, 2112, 3136
_O_QC, _O_KC, _O_VC, _O_GLR, _O_ZC, _O_GATES = 4160, 4672, 5184, 6208, 6224, 7248

P_GATES, P_ZA, P_UB, P_ZB, P_VC, P_ZC = 0, 6144, 7168, 8192, 9216, 10240
P_CQ, P_CKV, P_QC, P_KC, P_KR, P_GLR = 11264, 11776, 12288, 12800, 13312, 13440
P_TOTAL = 13824

VMEM_LIMIT = 56 * 1024 * 1024


def _cparams(n_axes, flags=None):
    return pltpu.CompilerParams(
        dimension_semantics=("arbitrary",) * n_axes, vmem_limit_bytes=VMEM_LIMIT, flags=flags)


def _silu(z):
    return z * (1.0 / (1.0 + jnp.exp(-z)))


def _rms(x, g):
    return x * lax.rsqrt(jnp.mean(x * x, axis=-1, keepdims=True) + EPS) * g


def _rmsnorm_kernel(x_ref, g_ref, o_ref):
    o_ref[...] = _rms(x_ref[...], g_ref[...]).astype(o_ref.dtype)


def _rmsnorm(x, g, tm):
    T, D = x.shape
    return pl.pallas_call(
        _rmsnorm_kernel,
        grid=(T // tm,),
        in_specs=[pl.BlockSpec((tm, D), lambda i: (i, 0)),
                  pl.BlockSpec((1, D), lambda i: (0, 0))],
        out_specs=pl.BlockSpec((tm, D), lambda i: (i, 0)),
        out_shape=jax.ShapeDtypeStruct((T, D), bf16),
        compiler_params=_cparams(1),
        name="rmsnorm",
    )(x, g)


def _inproj_kernel(h_ref, w_ref, o_ref):
    o_ref[...] = jnp.dot(h_ref[...], w_ref[...],
                         preferred_element_type=f32).astype(o_ref.dtype)


def _inproj(h, w, tm, tn):
    T, D = h.shape
    N = w.shape[1]
    return pl.pallas_call(
        _inproj_kernel,
        grid=(T // tm, N // tn),
        in_specs=[pl.BlockSpec((tm, D), lambda i, j: (i, 0)),
                  pl.BlockSpec((D, tn), lambda i, j: (0, j))],
        out_specs=pl.BlockSpec((tm, tn), lambda i, j: (i, j)),
        out_shape=jax.ShapeDtypeStruct((T, N), bf16),
        compiler_params=_cparams(2),
        name="inproj",
    )(h, w)


def _rope_kernel(pos_ref, inv_ref, sign_ref, cos_ref, sin_ref):
    ang = pos_ref[...].astype(f32) * inv_ref[...]
    cos_ref[...] = jnp.cos(ang)
    sin_ref[...] = jnp.sin(ang) * sign_ref[...]


def _rope_tables(pos, tm):
    T = pos.shape[0]
    half = MLA_ROPE // 2
    inv = 1.0 / (ROPE_THETA ** (jnp.arange(0, MLA_ROPE, 2, dtype=f32) / MLA_ROPE))
    zero = jnp.zeros((half,), f32)
    one = jnp.ones((half,), f32)
    inv_row = jnp.concatenate([inv, zero, inv, zero])[None, :]
    sign_row = jnp.concatenate([-one, zero, one, zero])[None, :]
    row = pl.BlockSpec((1, 128), lambda i: (0, 0))
    tab = pl.BlockSpec((tm, 128), lambda i: (i, 0))
    return pl.pallas_call(
        _rope_kernel,
        grid=(T // tm,),
        in_specs=[pl.BlockSpec((tm, 1), lambda i: (i, 0)), row, row],
        out_specs=[tab, tab],
        out_shape=[jax.ShapeDtypeStruct((T, 128), f32)] * 2,
        compiler_params=_cparams(1),
        name="rope_tables",
    )(pos, inv_row, sign_row)


def _mla_up_kernel(cq_ref, ckv_ref, kr_ref, cos_ref, sin_ref, gq_ref, gkv_ref,
                   wuq_ref, wuk_ref, wuvt_ref, q_ref, k_ref, vt_ref, *, qscale):
    cos = cos_ref[...]
    sin = sin_ref[...]

    def rope(x):
        return x * cos + pltpu.roll(x, 64, 1) * sin

    cqn = _rms(cq_ref[...].astype(f32), gq_ref[...]).astype(bf16)
    qf = jnp.dot(cqn, wuq_ref[...], preferred_element_type=f32) * qscale
    for h in range(MLA_HEADS):
        c0 = h * HEAD_PAD
        q_ref[0, h, 0, :, 0:128] = qf[:, c0:c0 + 128].astype(bf16)
        q_ref[0, h, 0, :, 128:256] = rope(qf[:, c0 + 128:c0 + 256]).astype(bf16)

    ckvn = _rms(ckv_ref[...].astype(f32), gkv_ref[...]).astype(bf16)
    kn = jnp.dot(ckvn, wuk_ref[...], preferred_element_type=f32)
    krr = rope(kr_ref[...].astype(f32)).astype(bf16)
    for h in range(MLA_HEADS):
        k_ref[0, h, 0, :, 0:128] = kn[:, h * 128:(h + 1) * 128].astype(bf16)
        k_ref[0, h, 0, :, 128:256] = krr

    vt = lax.dot_general(wuvt_ref[...], ckvn, (((1,), (1,)), ((), ())),
                         preferred_element_type=f32)
    ones = jnp.ones((VT_ROWS - MLA_V, vt.shape[1]), bf16)
    for h in range(MLA_HEADS):
        vt_ref[0, h, 0, 0:MLA_V, :] = vt[h * MLA_V:(h + 1) * MLA_V, :].astype(bf16)
        vt_ref[0, h, 0, MLA_V:VT_ROWS, :] = ones


def _mla_up(proj, cos_t, sin_t, gq, gkv, wuq, wuk, wuvt, B, S, blk):
    nb = S // blk
    H = MLA_HEADS

    def rows(b, j):
        return b * nb + j

    const = lambda b, j: (0, 0)
    qk_shape = jax.ShapeDtypeStruct((B, H, nb, blk, HEAD_PAD), bf16)
    qk_spec = pl.BlockSpec((1, H, 1, blk, HEAD_PAD), lambda b, j: (b, 0, j, 0, 0))
    qscale = (MLA_NOPE + MLA_ROPE) ** -0.5 * math.log2(math.e)
    return pl.pallas_call(
        functools.partial(_mla_up_kernel, qscale=qscale),
        grid=(B, nb),
        in_specs=[
            pl.BlockSpec((blk, 512), lambda b, j: (rows(b, j), P_CQ // 512)),
            pl.BlockSpec((blk, 512), lambda b, j: (rows(b, j), P_CKV // 512)),
            pl.BlockSpec((blk, 128), lambda b, j: (rows(b, j), P_KR // 128)),
            pl.BlockSpec((blk, 128), lambda b, j: (rows(b, j), 0)),
            pl.BlockSpec((blk, 128), lambda b, j: (rows(b, j), 0)),
            pl.BlockSpec((1, MLA_LORA), const),
            pl.BlockSpec((1, MLA_LORA), const),
            pl.BlockSpec(wuq.shape, const),
            pl.BlockSpec(wuk.shape, const),
            pl.BlockSpec(wuvt.shape, const),
        ],
        out_specs=[qk_spec, qk_spec,
                   pl.BlockSpec((1, H, 1, VT_ROWS, blk), lambda b, j: (b, 0, j, 0, 0))],
        out_shape=[qk_shape, qk_shape,
                   jax.ShapeDtypeStruct((B, H, nb, VT_ROWS, blk), bf16)],
        compiler_params=_cparams(2),
        name="mla_up",
    )(proj, proj, proj, cos_t, sin_t, gq, gkv, wuq, wuk, wuvt)


def _attn_kernel(q_ref, k_ref, vt_ref, z_ref, o_ref,
                 m_ref, al0_ref, al1_ref, mt0_ref, mt1_ref, acc_ref, s0_ref, s1_ref, *, tq, tk):
    i = pl.program_id(2)
    kpq = tq // tk
    assert kpq % 2 == 0
    n_full = i * kpq
    nt = (((1,), (1,)), ((), ()))
    m_ref[...] = jnp.full(m_ref.shape, -1e30, f32)
    acc_ref[...] = jnp.zeros(acc_ref.shape, f32)

    def scores(j, cols):
        return lax.dot_general(k_ref[0, 0, j], q_ref[0, 0, 0, cols, :], nt,
                               preferred_element_type=f32)

    def put_scores(s, bufs, cols):
        s_ref, mt_ref, _ = bufs
        s_ref[:, cols] = s
        mt_ref[:, cols] = jnp.max(s, axis=0, keepdims=True)

    def softmax_pv(j, bufs, cols):
        s_ref, mt_ref, al_ref = bufs
        m_old = m_ref[:, cols]
        m_new = jnp.maximum(m_old, mt_ref[:, cols])
        al_ref[:, cols] = jnp.exp2(m_old - m_new)
        m_ref[:, cols] = m_new
        p = jnp.exp2(s_ref[:, cols] - m_new).astype(bf16)
        acc_ref[:, cols] = (al_ref[:, cols] * acc_ref[:, cols]
                            + jnp.dot(vt_ref[0, 0, j], p, preferred_element_type=f32))

    everything = slice(0, tq)
    buf0, buf1 = (s0_ref, mt0_ref, al0_ref), (s1_ref, mt1_ref, al1_ref)
    put_scores(scores(0, everything), buf0, everything)

    def body(t, carry):
        j0 = kpq * t
        for u in range(kpq):
            put_scores(scores(j0 + u + 1, everything), (buf0, buf1)[(u + 1) % 2], everything)
            softmax_pv(j0 + u, (buf0, buf1)[u % 2], everything)
        return carry

    lax.fori_loop(0, i, body, 0)

    for kt in range(kpq):
        cols = slice(kt * tk, tq)
        bufs = buf0 if kt % 2 == 0 else buf1
        s = bufs[0][:, cols] if kt == 0 else scores(n_full + kt, cols)
        kc = lax.broadcasted_iota(jnp.int32, s.shape, 0) // CHUNK
        qc = lax.broadcasted_iota(jnp.int32, s.shape, 1) // CHUNK
        put_scores(jnp.where(kc <= qc, s, -1e30), bufs, cols)
        softmax_pv(n_full + kt, bufs, cols)

    o = acc_ref[0:MLA_V, :] / acc_ref[MLA_V:MLA_V + 1, :]
    z = z_ref[...].astype(f32)
    o_ref[...] = (o.T * _silu(z)).astype(o_ref.dtype)


def _attention(q, k, vt, proj, B, S, tq, tk):
    H = MLA_HEADS
    nq, nk = S // tq, S // tk
    T = B * S
    q = q.reshape(B, H, nq, tq, HEAD_PAD)
    row = pltpu.VMEM((1, tq), f32)
    return pl.pallas_call(
        functools.partial(_attn_kernel, tq=tq, tk=tk),
        grid=(B, H, nq),
        in_specs=[
            pl.BlockSpec((1, 1, 1, tq, HEAD_PAD), lambda b, h, i: (b, h, i, 0, 0)),
            pl.BlockSpec((1, 1, nk, tk, HEAD_PAD), lambda b, h, i: (b, h, 0, 0, 0)),
            pl.BlockSpec((1, 1, nk, VT_ROWS, tk), lambda b, h, i: (b, h, 0, 0, 0)),
            pl.BlockSpec((tq, MLA_V), lambda b, h, i: (b * nq + i, P_ZA // MLA_V + h)),
        ],
        out_specs=pl.BlockSpec((tq, MLA_V), lambda b, h, i: (b * nq + i, h)),
        out_shape=jax.ShapeDtypeStruct((T, BRANCH_WIDTH), bf16),
        scratch_shapes=[row, row, row, row, row, pltpu.VMEM((VT_ROWS, tq), f32),
                        pltpu.VMEM((tk, tq), f32), pltpu.VMEM((tk, tq), f32)],
        compiler_params=_cparams(3),
        name="mla_attention",
    )(q, k, vt, proj)


def _pool_kernel(u_ref, z_ref, w_ref, sc_ref, o_ref, ext_ref, *, tm):
    j = pl.program_id(1)
    H = POOL_HALO

    @pl.when(j == 0)
    def _():
        ext_ref[0:H, :] = jnp.zeros((H, ext_ref.shape[1]), f32)

    @pl.when(j > 0)
    def _():
        ext_ref[0:H, :] = ext_ref[tm:tm + H, :]

    u = u_ref[...].astype(f32)
    ext_ref[H:H + tm, :] = u
    t = j * tm + lax.broadcasted_iota(jnp.int32, (tm, 1), 0)
    for g, w in enumerate(POOL_WINDOWS):
        c0, c1 = g * POOL_GROUP_W, (g + 1) * POOL_GROUP_W
        ug = u[:, c0:c1]
        acc = ug
        for kk in range(1, w):
            acc = acc + ext_ref[H - kk:H - kk + tm, c0:c1]
        cnt = jnp.minimum(t + 1, w).astype(f32)
        m = acc / cnt - ug
        y = jnp.dot(m.astype(bf16), w_ref[g], preferred_element_type=f32) * sc_ref[:, c0:c1]
        o_ref[:, c0:c1] = (y * _silu(z_ref[:, c0:c1].astype(f32))).astype(o_ref.dtype)


def _pool(proj, pool_w, pool_scale, B, S, tm):
    nb = S // tm
    T = B * S
    W = BRANCH_WIDTH
    return pl.pallas_call(
        functools.partial(_pool_kernel, tm=tm),
        grid=(B, nb),
        in_specs=[
            pl.BlockSpec((tm, W), lambda b, j: (b * nb + j, P_UB // W)),
            pl.BlockSpec((tm, W), lambda b, j: (b * nb + j, P_ZB // W)),
            pl.BlockSpec(pool_w.shape, lambda b, j: (0, 0, 0)),
            pl.BlockSpec((1, W), lambda b, j: (0, 0)),
        ],
        out_specs=pl.BlockSpec((tm, W), lambda b, j: (b * nb + j, 0)),
        out_shape=jax.ShapeDtypeStruct((T, W), bf16),
        scratch_shapes=[pltpu.VMEM((tm + POOL_HALO, W), f32)],
        compiler_params=_cparams(2),
        name="pool",
    )(proj, proj, pool_w, pool_scale)


def _gla_kernel(q_ref, k_ref, v_ref, glr_ref, z_ref, wg_ref, bg_ref, ng_ref, o_ref,
                st_ref, *, tg):
    j = pl.program_id(1)

    @pl.when(j == 0)
    def _():
        st_ref[...] = jnp.zeros(st_ref.shape, f32)

    x = jnp.dot(glr_ref[...], wg_ref[...], preferred_element_type=f32) + bg_ref[...]
    log_a = -(jnp.maximum(-x, 0.0) + jnp.log1p(jnp.exp(-jnp.abs(x)))) / GLA_GATE_TAU

    r = lax.broadcasted_iota(jnp.int32, (tg, tg), 0)
    c = lax.broadcasted_iota(jnp.int32, (tg, tg), 1)
    tri = ((r // CHUNK == c // CHUNK) & (c <= r)).astype(bf16)
    hi = log_a.astype(bf16)
    rem = log_a - hi.astype(f32)
    mid = rem.astype(bf16)
    lo = (rem - mid.astype(f32)).astype(bf16)
    bcum = (jnp.dot(tri, hi, preferred_element_type=f32)
            + jnp.dot(tri, mid, preferred_element_type=f32)
            + jnp.dot(tri, lo, preferred_element_type=f32))

    q = q_ref[...].astype(f32) * (GLA_DK ** -0.5)
    k = k_ref[...].astype(f32)
    qd = (q * jnp.exp(bcum)).astype(bf16)
    kd = (k * jnp.exp(-bcum)).astype(bf16)
    rr = lax.broadcasted_iota(jnp.int32, (CHUNK, CHUNK), 0)
    cc = lax.broadcasted_iota(jnp.int32, (CHUNK, CHUNK), 1)
    causal = cc <= rr
    nt = (((1,), (1,)), ((), ()))
    for g in range(tg // CHUNK):
        r0, r1 = g * CHUNK, (g + 1) * CHUNK
        bl = bcum[r1 - 1:r1, :]
        ke = (k[r0:r1, :] * jnp.exp(bl - bcum[r0:r1, :])).astype(bf16)
        dec = jnp.exp(bl)
        for h in range(GLA_HEADS):
            k0, k1 = h * GLA_DK, (h + 1) * GLA_DK
            v0, v1 = h * GLA_DV, (h + 1) * GLA_DV
            v = v_ref[r0:r1, v0:v1]
            qh = qd[r0:r1, k0:k1]
            att = lax.dot_general(qh, kd[r0:r1, k0:k1], nt, preferred_element_type=f32)
            att = jnp.where(causal, att, 0.0).astype(bf16)
            st = st_ref[h]
            o = (jnp.dot(att, v, preferred_element_type=f32)
                 + lax.dot_general(qh, st.astype(bf16), nt, preferred_element_type=f32))
            ut = lax.dot_general(v, ke[:, k0:k1], (((0,), (0,)), ((), ())),
                                 preferred_element_type=f32)
            st_ref[h] = st * dec[:, k0:k1] + ut
            on = _rms(o, ng_ref[:, v0:v1])
            o_ref[r0:r1, v0:v1] = (on * _silu(z_ref[r0:r1, v0:v1].astype(f32))).astype(o_ref.dtype)


def _gla(proj, wg, bg, ng, B, S, tg):
    nb = S // tg
    T = B * S
    W = BRANCH_WIDTH
    QK = GLA_HEADS * GLA_DK
    rows = lambda b, j: b * nb + j
    return pl.pallas_call(
        functools.partial(_gla_kernel, tg=tg),
        grid=(B, nb),
        in_specs=[
            pl.BlockSpec((tg, QK), lambda b, j: (rows(b, j), P_QC // QK)),
            pl.BlockSpec((tg, QK), lambda b, j: (rows(b, j), P_KC // QK)),
            pl.BlockSpec((tg, W), lambda b, j: (rows(b, j), P_VC // W)),
            pl.BlockSpec((tg, 128), lambda b, j: (rows(b, j), P_GLR // 128)),
            pl.BlockSpec((tg, W), lambda b, j: (rows(b, j), P_ZC // W)),
            pl.BlockSpec(wg.shape, lambda b, j: (0, 0)),
            pl.BlockSpec((1, QK), lambda b, j: (0, 0)),
            pl.BlockSpec((1, W), lambda b, j: (0, 0)),
        ],
        out_specs=pl.BlockSpec((tg, W), lambda b, j: (rows(b, j), 0)),
        out_shape=jax.ShapeDtypeStruct((T, W), bf16),
        scratch_shapes=[pltpu.VMEM((GLA_HEADS, GLA_DV, GLA_DK), f32)],
        compiler_params=_cparams(2),
        name="gla",
    )(proj, proj, proj, proj, proj, wg, bg, ng)


def _merge_kernel(ya_ref, yb_ref, yc_ref, g0_ref, g1_ref, g2_ref, bm_ref, wb_ref, wo_ref,
                  x_ref, ng_ref, *out_refs, final):
    merged = None
    for n, (y_ref, g_ref) in enumerate(((ya_ref, g0_ref), (yb_ref, g1_ref), (yc_ref, g2_ref))):
        p = jnp.dot(y_ref[...], wb_ref[n], preferred_element_type=f32)
        gate = 1.0 / (1.0 + jnp.exp(-(g_ref[...].astype(f32) + bm_ref[n:n + 1, :])))
        merged = gate * p if merged is None else merged + gate * p
    xn = x_ref[...] + jnp.dot(merged.astype(bf16), wo_ref[...], preferred_element_type=f32)
    hn = _rms(xn, ng_ref[...])
    if final:
        out_refs[0][...] = hn
    else:
        out_refs[0][...] = xn
        out_refs[1][...] = hn.astype(bf16)


def _merge(ya, yb, yc, proj, bm, wb, wo, x, ng, tm, final):
    T, D = x.shape
    W = BRANCH_WIDTH
    row = lambda i: (i, 0)
    ytile = pl.BlockSpec((tm, W), row)
    xtile = pl.BlockSpec((tm, D), row)
    resident = dict(pipeline_mode=pl.Buffered(1))
    if final:
        out_specs = [xtile]
        out_shape = [jax.ShapeDtypeStruct((T, D), f32)]
    else:
        out_specs = [xtile, xtile]
        out_shape = [jax.ShapeDtypeStruct((T, D), f32), jax.ShapeDtypeStruct((T, D), bf16)]
    return pl.pallas_call(
        functools.partial(_merge_kernel, final=final),
        grid=(T // tm,),
        in_specs=[
            ytile, ytile, ytile,
            pl.BlockSpec((tm, D), lambda i: (i, P_GATES // D + 0)),
            pl.BlockSpec((tm, D), lambda i: (i, P_GATES // D + 1)),
            pl.BlockSpec((tm, D), lambda i: (i, P_GATES // D + 2)),
            pl.BlockSpec((N_BRANCH, D), lambda i: (0, 0)),
            pl.BlockSpec(wb.shape, lambda i: (0, 0, 0), **resident),
            pl.BlockSpec(wo.shape, lambda i: (0, 0), **resident),
            xtile,
            pl.BlockSpec((1, D), lambda i: (0, 0)),
        ],
        out_specs=out_specs,
        out_shape=out_shape,
        compiler_params=_cparams(1),
        name="merge_final" if final else "merge",
    )(ya, yb, yc, proj, proj, proj, bm, wb, wo, x, ng)


def _permute_w_in(w):
    D = w.shape[0]
    half = MLA_ROPE // 2
    sl = lambda o, n: w[:, o:o + n]
    z = lambda n: jnp.zeros((D, n), w.dtype)
    parts = [
        sl(_O_GATES, N_BRANCH * D_MODEL), sl(_O_ZA, 1024), sl(_O_UB, 1024), sl(_O_ZB, 1024),
        sl(_O_VC, 1024), sl(_O_ZC, 1024), sl(_O_CQ, 512), sl(_O_CKV, 512),
        sl(_O_QC, 512), sl(_O_KC, 512),
        sl(_O_KR, half), z(half), sl(_O_KR + half, half), z(half),
        sl(_O_GLR, GLA_GATE_RANK), z(128 - GLA_GATE_RANK),
    ]
    used = P_GLR + 128
    parts.append(z(P_TOTAL - used))
    return jnp.concatenate(parts, axis=1).astype(bf16)


def _permute_w_uq(w):
    half = MLA_ROPE // 2
    w = w.reshape(MLA_LORA, MLA_HEADS, MLA_NOPE + MLA_ROPE)
    z = jnp.zeros((MLA_LORA, MLA_HEADS, half), w.dtype)
    out = jnp.concatenate([w[..., :MLA_NOPE], w[..., MLA_NOPE:MLA_NOPE + half], z,
                           w[..., MLA_NOPE + half:], z], axis=-1)
    return out.reshape(MLA_LORA, MLA_HEADS * HEAD_PAD).astype(bf16)


def _split_w_ukv(w):
    w = w.reshape(MLA_LORA, MLA_HEADS, MLA_NOPE + MLA_V)
    wuk = w[..., :MLA_NOPE].reshape(MLA_LORA, MLA_HEADS * MLA_NOPE).astype(bf16)
    wuvt = w[..., MLA_NOPE:].reshape(MLA_LORA, MLA_HEADS * MLA_V).T.astype(bf16)
    return wuk, wuvt


def _pick(n, pref):
    return pref if n % pref == 0 else n


def kernel(x, positions, norm_g, w_in, b_merge, mla_q_norm, mla_kv_norm, mla_w_uq, mla_w_ukv,
           pool_w, pool_scale, gla_w_gate, gla_b_gate, gla_norm, w_branch, w_out, final_norm):
    B, S, D = x.shape
    assert D == D_MODEL and S % 2048 == 0
    T = B * S
    depth = norm_g.shape[0]
    blk = 512
    tq_attn = 2048
    tm_proj = _pick(T, 1024)
    tn_proj = 1536
    tm_pool = 512
    tg = 256
    tm_merge = 256

    xf = x.reshape(T, D)
    cos_t, sin_t = _rope_tables(positions.reshape(T, 1), 512)
    h = _rmsnorm(xf, norm_g[0][None, :], 512)
    out = None
    for l in range(depth):
        w_in_p = _permute_w_in(w_in[l])
        wuq = _permute_w_uq(mla_w_uq[l])
        wuk, wuvt = _split_w_ukv(mla_w_ukv[l])
        wg = jnp.zeros((128, GLA_HEADS * GLA_DK), bf16).at[:GLA_GATE_RANK].set(
            gla_w_gate[l].astype(bf16))

        proj = _inproj(h, w_in_p, tm_proj, tn_proj)
        q, k, vt = _mla_up(proj, cos_t, sin_t, mla_q_norm[l][None, :], mla_kv_norm[l][None, :],
                           wuq, wuk, wuvt, B, S, blk)
        ya = _attention(q, k, vt, proj, B, S, tq_attn, blk)
        yb = _pool(proj, pool_w[l].astype(bf16), pool_scale[l][None, :], B, S, tm_pool)
        yc = _gla(proj, wg, gla_b_gate[l][None, :], gla_norm[l][None, :], B, S, tg)

        final = l == depth - 1
        next_g = final_norm if final else norm_g[l + 1]
        res = _merge(ya, yb, yc, proj, b_merge[l], w_branch[l].astype(bf16),
                     w_out[l].astype(bf16), xf, next_g[None, :], tm_merge, final)
        if final:
            out = res[0]
        else:
            xf, h = res
    return out.reshape(B, S, D)
```

```python
import functools
import math

import jax
import jax.numpy as jnp
from jax import lax
from jax.experimental import pallas as pl
from jax.experimental.pallas import tpu as pltpu

f32 = jnp.float32
bf16 = jnp.bfloat16

D_MODEL = 2048
CHUNK = 64
EPS = 1e-6

MLA_HEADS = 8
MLA_NOPE = 128
MLA_ROPE = 64
MLA_V = 128
VT_ROWS = MLA_V + 16
FAST_MAX_JUMP = 24.0
MLA_LORA = 512
ROPE_THETA = 10000.0
HEAD_PAD = 256

POOL_WINDOWS = (2, 4, 8, 16)
POOL_GROUP_W = 256
POOL_HALO = 16

GLA_HEADS = 4
GLA_DK = 128
GLA_DV = 256
GLA_GATE_RANK = 16
GLA_GATE_TAU = 16.0

N_BRANCH = 3
BRANCH_WIDTH = 1024

_O_CQ, _O_CKV, _O_KR, _O_ZA, _O_UB, _O_ZB = 0, 512, 1024, 1088, 2112, 3136
_O_QC, _O_KC, _O_VC, _O_GLR, _O_ZC, _O_GATES = 4160, 4672, 5184, 6208, 6224, 7248

P_GATES, P_ZA, P_UB, P_ZB, P_VC, P_ZC = 0, 6144, 7168, 8192, 9216, 10240
P_CQ, P_CKV, P_QC, P_KC, P_KR, P_GLR = 11264, 11776, 12288, 12800, 13312, 13440
P_TOTAL = 13824

VMEM_LIMIT = 56 * 1024 * 1024


def _cparams(n_axes, flags=None):
    return pltpu.CompilerParams(
        dimension_semantics=("arbitrary",) * n_axes, vmem_limit_bytes=VMEM_LIMIT, flags=flags)


def _silu(z):
    return z * (1.0 / (1.0 + jnp.exp(-z)))


def _rms(x, g):
    return x * lax.rsqrt(jnp.mean(x * x, axis=-1, keepdims=True) + EPS) * g


def _rmsnorm_kernel(x_ref, g_ref, o_ref):
    o_ref[...] = _rms(x_ref[...], g_ref[...]).astype(o_ref.dtype)


def _rmsnorm(x, g, tm):
    T, D = x.shape
    return pl.pallas_call(
        _rmsnorm_kernel,
        grid=(T // tm,),
        in_specs=[pl.BlockSpec((tm, D), lambda i: (i, 0)),
                  pl.BlockSpec((1, D), lambda i: (0, 0))],
        out_specs=pl.BlockSpec((tm, D), lambda i: (i, 0)),
        out_shape=jax.ShapeDtypeStruct((T, D), bf16),
        compiler_params=_cparams(1),
        name="rmsnorm",
    )(x, g)


def _inproj_kernel(h_ref, w_ref, o_ref):
    o_ref[...] = jnp.dot(h_ref[...], w_ref[...],
                         preferred_element_type=f32).astype(o_ref.dtype)


def _inproj(h, w, tm, tn):
    T, D = h.shape
    N = w.shape[1]
    return pl.pallas_call(
        _inproj_kernel,
        grid=(T // tm, N // tn),
        in_specs=[pl.BlockSpec((tm, D), lambda i, j: (i, 0)),
                  pl.BlockSpec((D, tn), lambda i, j: (0, j))],
        out_specs=pl.BlockSpec((tm, tn), lambda i, j: (i, j)),
        out_shape=jax.ShapeDtypeStruct((T, N), bf16),
        compiler_params=_cparams(2),
        name="inproj",
    )(h, w)


def _rope_kernel(pos_ref, inv_ref, sign_ref, cos_ref, sin_ref):
    ang = pos_ref[...].astype(f32) * inv_ref[...]
    cos_ref[...] = jnp.cos(ang)
    sin_ref[...] = jnp.sin(ang) * sign_ref[...]


def _rope_tables(pos, tm):
    T = pos.shape[0]
    half = MLA_ROPE // 2
    inv = 1.0 / (ROPE_THETA ** (jnp.arange(0, MLA_ROPE, 2, dtype=f32) / MLA_ROPE))
    zero = jnp.zeros((half,), f32)
    one = jnp.ones((half,), f32)
    inv_row = jnp.concatenate([inv, zero, inv, zero])[None, :]
    sign_row = jnp.concatenate([-one, zero, one, zero])[None, :]
    row = pl.BlockSpec((1, 128), lambda i: (0, 0))
    tab = pl.BlockSpec((tm, 128), lambda i: (i, 0))
    return pl.pallas_call(
        _rope_kernel,
        grid=(T // tm,),
        in_specs=[pl.BlockSpec((tm, 1), lambda i: (i, 0)), row, row],
        out_specs=[tab, tab],
        out_shape=[jax.ShapeDtypeStruct((T, 128), f32)] * 2,
        compiler_params=_cparams(1),
        name="rope_tables",
    )(pos, inv_row, sign_row)


def _mla_up_kernel(cq_ref, ckv_ref, kr_ref, cos_ref, sin_ref, gq_ref, gkv_ref,
                   wuq_ref, wuk_ref, wuvt_ref, q_ref, k_ref, vt_ref, *, qscale):
    cos = cos_ref[...]
    sin = sin_ref[...]

    def rope(x):
        return x * cos + pltpu.roll(x, 64, 1) * sin

    cqn = _rms(cq_ref[...].astype(f32), gq_ref[...]).astype(bf16)
    qf = jnp.dot(cqn, wuq_ref[...], preferred_element_type=f32) * qscale
    for h in range(MLA_HEADS):
        c0 = h * HEAD_PAD
        q_ref[0, h, 0, :, 0:128] = qf[:, c0:c0 + 128].astype(bf16)
        q_ref[0, h, 0, :, 128:256] = rope(qf[:, c0 + 128:c0 + 256]).astype(bf16)

    ckvn = _rms(ckv_ref[...].astype(f32), gkv_ref[...]).astype(bf16)
    kn = jnp.dot(ckvn, wuk_ref[...], preferred_element_type=f32)
    krr = rope(kr_ref[...].astype(f32)).astype(bf16)
    for h in range(MLA_HEADS):
        k_ref[0, h, 0, :, 0:128] = kn[:, h * 128:(h + 1) * 128].astype(bf16)
        k_ref[0, h, 0, :, 128:256] = krr

    vt = lax.dot_general(wuvt_ref[...], ckvn, (((1,), (1,)), ((), ())),
                         preferred_element_type=f32)
    ones = jnp.ones((VT_ROWS - MLA_V, vt.shape[1]), bf16)
    for h in range(MLA_HEADS):
        vt_ref[0, h, 0, 0:MLA_V, :] = vt[h * MLA_V:(h + 1) * MLA_V, :].astype(bf16)
        vt_ref[0, h, 0, MLA_V:VT_ROWS, :] = ones


def _mla_up(proj, cos_t, sin_t, gq, gkv, wuq, wuk, wuvt, B, S, blk):
    nb = S // blk
    H = MLA_HEADS

    def rows(b, j):
        return b * nb + j

    const = lambda b, j: (0, 0)
    qk_shape = jax.ShapeDtypeStruct((B, H, nb, blk, HEAD_PAD), bf16)
    qk_spec = pl.BlockSpec((1, H, 1, blk, HEAD_PAD), lambda b, j: (b, 0, j, 0, 0))
    qscale = (MLA_NOPE + MLA_ROPE) ** -0.5 * math.log2(math.e)
    return pl.pallas_call(
        functools.partial(_mla_up_kernel, qscale=qscale),
        grid=(B, nb),
        in_specs=[
            pl.BlockSpec((blk, 512), lambda b, j: (rows(b, j), P_CQ // 512)),
            pl.BlockSpec((blk, 512), lambda b, j: (rows(b, j), P_CKV // 512)),
            pl.BlockSpec((blk, 128), lambda b, j: (rows(b, j), P_KR // 128)),
            pl.BlockSpec((blk, 128), lambda b, j: (rows(b, j), 0)),
            pl.BlockSpec((blk, 128), lambda b, j: (rows(b, j), 0)),
            pl.BlockSpec((1, MLA_LORA), const),
            pl.BlockSpec((1, MLA_LORA), const),
            pl.BlockSpec(wuq.shape, const),
            pl.BlockSpec(wuk.shape, const),
            pl.BlockSpec(wuvt.shape, const),
        ],
        out_specs=[qk_spec, qk_spec,
                   pl.BlockSpec((1, H, 1, VT_ROWS, blk), lambda b, j: (b, 0, j, 0, 0))],
        out_shape=[qk_shape, qk_shape,
                   jax.ShapeDtypeStruct((B, H, nb, VT_ROWS, blk), bf16)],
        compiler_params=_cparams(2),
        name="mla_up",
    )(proj, proj, proj, cos_t, sin_t, gq, gkv, wuq, wuk, wuvt)


def _attn_kernel(q_ref, k_ref, vt_ref, z_ref, o_ref,
                 m_ref, r_ref, risk_ref, al0_ref, al1_ref, mt0_ref, mt1_ref,
                 acc_ref, accd_ref, md_ref, s0_ref, s1_ref, p0_ref, p1_ref, *, tq, tk):
    i = pl.program_id(2)
    kpq = tq // tk
    assert kpq % 2 == 0
    n_full = i * kpq
    nt = (((1,), (1,)), ((), ()))
    m_ref[...] = jnp.full(m_ref.shape, -1e30, f32)
    acc_ref[...] = jnp.zeros(acc_ref.shape, f32)

    def scores(j, cols):
        return lax.dot_general(k_ref[0, 0, j], q_ref[0, 0, 0, cols, :], nt,
                               preferred_element_type=f32)

    def put_scores(s, bufs, cols):
        s_ref, mt_ref, _ = bufs
        s_ref[:, cols] = s
        mt_ref[:, cols] = jnp.max(s, axis=0, keepdims=True)

    def softmax_pv(j, bufs, cols):
        s_ref, mt_ref, al_ref = bufs
        m_old = m_ref[:, cols]
        m_new = jnp.maximum(m_old, mt_ref[:, cols])
        al_ref[:, cols] = jnp.exp2(m_old - m_new)
        m_ref[:, cols] = m_new
        p = jnp.exp2(s_ref[:, cols] - m_new).astype(bf16)
        acc_ref[:, cols] = (al_ref[:, cols] * acc_ref[:, cols]
                            + jnp.dot(vt_ref[0, 0, j], p, preferred_element_type=f32))

    everything = slice(0, tq)
    buf0, buf1 = (s0_ref, mt0_ref, al0_ref), (s1_ref, mt1_ref, al1_ref)

    for kt in range(kpq):
        cols = slice(kt * tk, tq)
        bufs = buf0 if kt % 2 == 0 else buf1
        s = scores(n_full + kt, cols)
        kc = lax.broadcasted_iota(jnp.int32, s.shape, 0) // CHUNK
        qc = lax.broadcasted_iota(jnp.int32, s.shape, 1) // CHUNK
        put_scores(jnp.where(kc <= qc, s, -1e30), bufs, cols)
        softmax_pv(n_full + kt, bufs, cols)

    accd_ref[...] = acc_ref[...]
    md_ref[...] = m_ref[...]
    r_ref[...] = m_ref[...]
    risk_ref[...] = jnp.zeros(risk_ref.shape, f32)
    p1_ref[...] = jnp.zeros(p1_ref.shape, bf16)
    al1_ref[...] = jnp.ones(al1_ref.shape, f32)
    fast = ((p0_ref, al0_ref), (p1_ref, al1_ref))

    def fast_pv(j, slot):
        p_ref, al_ref = fast[slot]
        acc_ref[...] = al_ref[...] * acc_ref[...] + jnp.dot(vt_ref[0, 0, j], p_ref[...],
                                                             preferred_element_type=f32)

    def fast_step(j, slot):
        p_ref, al_ref = fast[slot]
        ref = m_ref[...]
        s = scores(j, everything)
        mt = jnp.max(s, axis=0, keepdims=True)
        p_ref[...] = jnp.exp2(s - ref).astype(bf16)
        al_ref[...] = jnp.exp2(r_ref[...] - ref)
        r_ref[...] = ref
        risk_ref[...] = jnp.maximum(risk_ref[...], mt - ref)
        m_ref[...] = jnp.maximum(ref, mt)
        fast_pv(jnp.maximum(j - 1, 0), 1 - slot)

    def fast_body(t, carry):
        for u in range(kpq):
            fast_step(kpq * t + u, u % 2)
        return carry

    lax.fori_loop(0, i, fast_body, 0)
    fast_pv(jnp.maximum(n_full - 1, 0), 1)

    @pl.when(jnp.max(risk_ref[...]) > FAST_MAX_JUMP)
    def _():
        acc_ref[...] = accd_ref[...]
        m_ref[...] = md_ref[...]
        put_scores(scores(0, everything), buf0, everything)

        def body(t, carry):
            for u in range(kpq):
                j = kpq * t + u
                put_scores(scores(j + 1, everything), (buf0, buf1)[(u + 1) % 2], everything)
                softmax_pv(j, (buf0, buf1)[u % 2], everything)
            return carry

        lax.fori_loop(0, i, body, 0)

    o = acc_ref[0:MLA_V, :] / acc_ref[MLA_V:MLA_V + 1, :]
    z = z_ref[...].astype(f32)
    o_ref[...] = (o.T * _silu(z)).astype(o_ref.dtype)


def _attention(q, k, vt, proj, B, S, tq, tk):
    H = MLA_HEADS
    nq, nk = S // tq, S // tk
    T = B * S
    q = q.reshape(B, H, nq, tq, HEAD_PAD)
    row = pltpu.VMEM((1, tq), f32)
    acc = pltpu.VMEM((VT_ROWS, tq), f32)
    return pl.pallas_call(
        functools.partial(_attn_kernel, tq=tq, tk=tk),
        grid=(B, H, nq),
        in_specs=[
            pl.BlockSpec((1, 1, 1, tq, HEAD_PAD), lambda b, h, i: (b, h, i, 0, 0)),
            pl.BlockSpec((1, 1, nk, tk, HEAD_PAD), lambda b, h, i: (b, h, 0, 0, 0)),
            pl.BlockSpec((1, 1, nk, VT_ROWS, tk), lambda b, h, i: (b, h, 0, 0, 0)),
            pl.BlockSpec((tq, MLA_V), lambda b, h, i: (b * nq + i, P_ZA // MLA_V + h)),
        ],
        out_specs=pl.BlockSpec((tq, MLA_V), lambda b, h, i: (b * nq + i, h)),
        out_shape=jax.ShapeDtypeStruct((T, BRANCH_WIDTH), bf16),
        scratch_shapes=[row] * 7 + [acc, acc, row,
                                    pltpu.VMEM((tk, tq), f32), pltpu.VMEM((tk, tq), f32),
                                    pltpu.VMEM((tk, tq), bf16), pltpu.VMEM((tk, tq), bf16)],
        compiler_params=_cparams(3),
        name="mla_attention",
    )(q, k, vt, proj)


def _pool_kernel(u_ref, z_ref, w_ref, sc_ref, o_ref, ext_ref, *, tm):
    j = pl.program_id(1)
    H = POOL_HALO

    @pl.when(j == 0)
    def _():
        ext_ref[0:H, :] = jnp.zeros((H, ext_ref.shape[1]), f32)

    @pl.when(j > 0)
    def _():
        ext_ref[0:H, :] = ext_ref[tm:tm + H, :]

    u = u_ref[...].astype(f32)
    ext_ref[H:H + tm, :] = u
    t = j * tm + lax.broadcasted_iota(jnp.int32, (tm, 1), 0)
    for g, w in enumerate(POOL_WINDOWS):
        c0, c1 = g * POOL_GROUP_W, (g + 1) * POOL_GROUP_W
        ug = u[:, c0:c1]
        acc = ug
        for kk in range(1, w):
            acc = acc + ext_ref[H - kk:H - kk + tm, c0:c1]
        cnt = jnp.minimum(t + 1, w).astype(f32)
        m = acc / cnt - ug
        y = jnp.dot(m.astype(bf16), w_ref[g], preferred_element_type=f32) * sc_ref[:, c0:c1]
        o_ref[:, c0:c1] = (y * _silu(z_ref[:, c0:c1].astype(f32))).astype(o_ref.dtype)


def _pool(proj, pool_w, pool_scale, B, S, tm):
    nb = S // tm
    T = B * S
    W = BRANCH_WIDTH
    return pl.pallas_call(
        functools.partial(_pool_kernel, tm=tm),
        grid=(B, nb),
        in_specs=[
            pl.BlockSpec((tm, W), lambda b, j: (b * nb + j, P_UB // W)),
            pl.BlockSpec((tm, W), lambda b, j: (b * nb + j, P_ZB // W)),
            pl.BlockSpec(pool_w.shape, lambda b, j: (0, 0, 0)),
            pl.BlockSpec((1, W), lambda b, j: (0, 0)),
        ],
        out_specs=pl.BlockSpec((tm, W), lambda b, j: (b * nb + j, 0)),
        out_shape=jax.ShapeDtypeStruct((T, W), bf16),
        scratch_shapes=[pltpu.VMEM((tm + POOL_HALO, W), f32)],
        compiler_params=_cparams(2),
        name="pool",
    )(proj, proj, pool_w, pool_scale)


def _gla_kernel(q_ref, k_ref, v_ref, glr_ref, z_ref, wg_ref, bg_ref, ng_ref, o_ref,
                st_ref, *, tg):
    j = pl.program_id(1)

    @pl.when(j == 0)
    def _():
        st_ref[...] = jnp.zeros(st_ref.shape, f32)

    x = jnp.dot(glr_ref[...], wg_ref[...], preferred_element_type=f32) + bg_ref[...]
    log_a = -(jnp.maximum(-x, 0.0) + jnp.log1p(jnp.exp(-jnp.abs(x)))) / GLA_GATE_TAU

    r = lax.broadcasted_iota(jnp.int32, (tg, tg), 0)
    c = lax.broadcasted_iota(jnp.int32, (tg, tg), 1)
    tri = ((r // CHUNK == c // CHUNK) & (c <= r)).astype(bf16)
    hi = log_a.astype(bf16)
    rem = log_a - hi.astype(f32)
    mid = rem.astype(bf16)
    lo = (rem - mid.astype(f32)).astype(bf16)
    bcum = (jnp.dot(tri, hi, preferred_element_type=f32)
            + jnp.dot(tri, mid, preferred_element_type=f32)
            + jnp.dot(tri, lo, preferred_element_type=f32))

    q = q_ref[...].astype(f32) * (GLA_DK ** -0.5)
    k = k_ref[...].astype(f32)
    qd = (q * jnp.exp(bcum)).astype(bf16)
    kd = (k * jnp.exp(-bcum)).astype(bf16)
    rr = lax.broadcasted_iota(jnp.int32, (CHUNK, CHUNK), 0)
    cc = lax.broadcasted_iota(jnp.int32, (CHUNK, CHUNK), 1)
    causal = cc <= rr
    nt = (((1,), (1,)), ((), ()))
    for g in range(tg // CHUNK):
        r0, r1 = g * CHUNK, (g + 1) * CHUNK
        bl = bcum[r1 - 1:r1, :]
        ke = (k[r0:r1, :] * jnp.exp(bl - bcum[r0:r1, :])).astype(bf16)
        dec = jnp.exp(bl)
        for h in range(GLA_HEADS):
            k0, k1 = h * GLA_DK, (h + 1) * GLA_DK
            v0, v1 = h * GLA_DV, (h + 1) * GLA_DV
            v = v_ref[r0:r1, v0:v1]
            qh = qd[r0:r1, k0:k1]
            att = lax.dot_general(qh, kd[r0:r1, k0:k1], nt, preferred_element_type=f32)
            att = jnp.where(causal, att, 0.0).astype(bf16)
            st = st_ref[h]
            o = (jnp.dot(att, v, preferred_element_type=f32)
                 + lax.dot_general(qh, st.astype(bf16), nt, preferred_element_type=f32))
            ut = lax.dot_general(v, ke[:, k0:k1], (((0,), (0,)), ((), ())),
                                 preferred_element_type=f32)
            st_ref[h] = st * dec[:, k0:k1] + ut
            on = _rms(o, ng_ref[:, v0:v1])
            o_ref[r0:r1, v0:v1] = (on * _silu(z_ref[r0:r1, v0:v1].astype(f32))).astype(o_ref.dtype)


def _gla(proj, wg, bg, ng, B, S, tg):
    nb = S // tg
    T = B * S
    W = BRANCH_WIDTH
    QK = GLA_HEADS * GLA_DK
    rows = lambda b, j: b * nb + j
    return pl.pallas_call(
        functools.partial(_gla_kernel, tg=tg),
        grid=(B, nb),
        in_specs=[
            pl.BlockSpec((tg, QK), lambda b, j: (rows(b, j), P_QC // QK)),
            pl.BlockSpec((tg, QK), lambda b, j: (rows(b, j), P_KC // QK)),
            pl.BlockSpec((tg, W), lambda b, j: (rows(b, j), P_VC // W)),
            pl.BlockSpec((tg, 128), lambda b, j: (rows(b, j), P_GLR // 128)),
            pl.BlockSpec((tg, W), lambda b, j: (rows(b, j), P_ZC // W)),
            pl.BlockSpec(wg.shape, lambda b, j: (0, 0)),
            pl.BlockSpec((1, QK), lambda b, j: (0, 0)),
            pl.BlockSpec((1, W), lambda b, j: (0, 0)),
        ],
        out_specs=pl.BlockSpec((tg, W), lambda b, j: (rows(b, j), 0)),
        out_shape=jax.ShapeDtypeStruct((T, W), bf16),
        scratch_shapes=[pltpu.VMEM((GLA_HEADS, GLA_DV, GLA_DK), f32)],
        compiler_params=_cparams(2),
        name="gla",
    )(proj, proj, proj, proj, proj, wg, bg, ng)


def _merge_kernel(ya_ref, yb_ref, yc_ref, g0_ref, g1_ref, g2_ref, bm_ref, wb_ref, wo_ref,
                  x_ref, ng_ref, *out_refs, final):
    merged = None
    for n, (y_ref, g_ref) in enumerate(((ya_ref, g0_ref), (yb_ref, g1_ref), (yc_ref, g2_ref))):
        p = jnp.dot(y_ref[...], wb_ref[n], preferred_element_type=f32)
        gate = 1.0 / (1.0 + jnp.exp(-(g_ref[...].astype(f32) + bm_ref[n:n + 1, :])))
        merged = gate * p if merged is None else merged + gate * p
    xn = x_ref[...] + jnp.dot(merged.astype(bf16), wo_ref[...], preferred_element_type=f32)
    hn = _rms(xn, ng_ref[...])
    if final:
        out_refs[0][...] = hn
    else:
        out_refs[0][...] = xn
        out_refs[1][...] = hn.astype(bf16)


def _merge(ya, yb, yc, proj, bm, wb, wo, x, ng, tm, final):
    T, D = x.shape
    W = BRANCH_WIDTH
    row = lambda i: (i, 0)
    ytile = pl.BlockSpec((tm, W), row)
    xtile = pl.BlockSpec((tm, D), row)
    resident = dict(pipeline_mode=pl.Buffered(1))
    if final:
        out_specs = [xtile]
        out_shape = [jax.ShapeDtypeStruct((T, D), f32)]
    else:
        out_specs = [xtile, xtile]
        out_shape = [jax.ShapeDtypeStruct((T, D), f32), jax.ShapeDtypeStruct((T, D), bf16)]
    return pl.pallas_call(
        functools.partial(_merge_kernel, final=final),
        grid=(T // tm,),
        in_specs=[
            ytile, ytile, ytile,
            pl.BlockSpec((tm, D), lambda i: (i, P_GATES // D + 0)),
            pl.BlockSpec((tm, D), lambda i: (i, P_GATES // D + 1)),
            pl.BlockSpec((tm, D), lambda i: (i, P_GATES // D + 2)),
            pl.BlockSpec((N_BRANCH, D), lambda i: (0, 0)),
            pl.BlockSpec(wb.shape, lambda i: (0, 0, 0), **resident),
            pl.BlockSpec(wo.shape, lambda i: (0, 0), **resident),
            xtile,
            pl.BlockSpec((1, D), lambda i: (0, 0)),
        ],
        out_specs=out_specs,
        out_shape=out_shape,
        compiler_params=_cparams(1),
        name="merge_final" if final else "merge",
    )(ya, yb, yc, proj, proj, proj, bm, wb, wo, x, ng)


def _permute_w_in(w):
    D = w.shape[0]
    half = MLA_ROPE // 2
    sl = lambda o, n: w[:, o:o + n]
    z = lambda n: jnp.zeros((D, n), w.dtype)
    parts = [
        sl(_O_GATES, N_BRANCH * D_MODEL), sl(_O_ZA, 1024), sl(_O_UB, 1024), sl(_O_ZB, 1024),
        sl(_O_VC, 1024), sl(_O_ZC, 1024), sl(_O_CQ, 512), sl(_O_CKV, 512),
        sl(_O_QC, 512), sl(_O_KC, 512),
        sl(_O_KR, half), z(half), sl(_O_KR + half, half), z(half),
        sl(_O_GLR, GLA_GATE_RANK), z(128 - GLA_GATE_RANK),
    ]
    used = P_GLR + 128
    parts.append(z(P_TOTAL - used))
    return jnp.concatenate(parts, axis=1).astype(bf16)


def _permute_w_uq(w):
    half = MLA_ROPE // 2
    w = w.reshape(MLA_LORA, MLA_HEADS, MLA_NOPE + MLA_ROPE)
    z = jnp.zeros((MLA_LORA, MLA_HEADS, half), w.dtype)
    out = jnp.concatenate([w[..., :MLA_NOPE], w[..., MLA_NOPE:MLA_NOPE + half], z,
                           w[..., MLA_NOPE + half:], z], axis=-1)
    return out.reshape(MLA_LORA, MLA_HEADS * HEAD_PAD).astype(bf16)


def _split_w_ukv(w):
    w = w.reshape(MLA_LORA, MLA_HEADS, MLA_NOPE + MLA_V)
    wuk = w[..., :MLA_NOPE].reshape(MLA_LORA, MLA_HEADS * MLA_NOPE).astype(bf16)
    wuvt = w[..., MLA_NOPE:].reshape(MLA_LORA, MLA_HEADS * MLA_V).T.astype(bf16)
    return wuk, wuvt


def _pick(n, pref):
    return pref if n % pref == 0 else n


def kernel(x, positions, norm_g, w_in, b_merge, mla_q_norm, mla_kv_norm, mla_w_uq, mla_w_ukv,
           pool_w, pool_scale, gla_w_gate, gla_b_gate, gla_norm, w_branch, w_out, final_norm):
    B, S, D = x.shape
    assert D == D_MODEL and S % 2048 == 0
    T = B * S
    depth = norm_g.shape[0]
    blk = 512
    tq_attn = 2048
    tm_proj = _pick(T, 1024)
    tn_proj = 1536
    tm_pool = 512
    tg = 256
    tm_merge = 256

    xf = x.reshape(T, D)
    cos_t, sin_t = _rope_tables(positions.reshape(T, 1), 512)
    h = _rmsnorm(xf, norm_g[0][None, :], 512)
    out = None
    for l in range(depth):
        w_in_p = _permute_w_in(w_in[l])
        wuq = _permute_w_uq(mla_w_uq[l])
        wuk, wuvt = _split_w_ukv(mla_w_ukv[l])
        wg = jnp.zeros((128, GLA_HEADS * GLA_DK), bf16).at[:GLA_GATE_RANK].set(
            gla_w_gate[l].astype(bf16))

        proj = _inproj(h, w_in_p, tm_proj, tn_proj)
        q, k, vt = _mla_up(proj, cos_t, sin_t, mla_q_norm[l][None, :], mla_kv_norm[l][None, :],
                           wuq, wuk, wuvt, B, S, blk)
        ya = _attention(q, k, vt, proj, B, S, tq_attn, blk)
        yb = _pool(proj, pool_w[l].astype(bf16), pool_scale[l][None, :], B, S, tm_pool)
        yc = _gla(proj, wg, gla_b_gate[l][None, :], gla_norm[l][None, :], B, S, tg)

        final = l == depth - 1
        next_g = final_norm if final else norm_g[l + 1]
        res = _merge(ya, yb, yc, proj, b_merge[l], w_branch[l].astype(bf16),
                     w_out[l].astype(bf16), xf, next_g[None, :], tm_merge, final)
        if final:
            out = res[0]
        else:
            xf, h = res
    return out.reshape(B, S, D)
```

```python
import functools
import math

import jax
import jax.numpy as jnp
from jax import lax
from jax.experimental import pallas as pl
from jax.experimental.pallas import tpu as pltpu

f32 = jnp.float32
bf16 = jnp.bfloat16

D_MODEL = 2048
CHUNK = 64
EPS = 1e-6

MLA_HEADS = 8
MLA_NOPE = 128
MLA_ROPE = 64
MLA_V = 128
VT_ROWS = MLA_V + 16
FAST_MAX_JUMP = 24.0
MLA_LORA = 512
ROPE_THETA = 10000.0
HEAD_PAD = 256

POOL_WINDOWS = (2, 4, 8, 16)
POOL_GROUP_W = 256
POOL_HALO = 16

GLA_HEADS = 4
GLA_DK = 128
GLA_DV = 256
GLA_GATE_RANK = 16
GLA_GATE_TAU = 16.0

N_BRANCH = 3
BRANCH_WIDTH = 1024

_O_CQ, _O_CKV, _O_KR, _O_ZA, _O_UB, _O_ZB = 0, 512, 1024, 1088, 2112, 3136
_O_QC, _O_KC, _O_VC, _O_GLR, _O_ZC, _O_GATES = 4160, 4672, 5184, 6208, 6224, 7248

P_GATES, P_ZA, P_UB, P_ZB, P_VC, P_ZC = 0, 6144, 7168, 8192, 9216, 10240
P_CQ, P_CKV, P_QC, P_KC, P_KR, P_GLR = 11264, 11776, 12288, 12800, 13312, 13440
P_TOTAL = 13824

VMEM_LIMIT = 56 * 1024 * 1024


def _cparams(n_axes, flags=None):
    return pltpu.CompilerParams(
        dimension_semantics=("arbitrary",) * n_axes, vmem_limit_bytes=VMEM_LIMIT, flags=flags)


def _silu(z):
    return z * (1.0 / (1.0 + jnp.exp(-z)))


def _rms(x, g):
    return x * lax.rsqrt(jnp.mean(x * x, axis=-1, keepdims=True) + EPS) * g


def _rmsnorm_kernel(x_ref, g_ref, o_ref):
    o_ref[...] = _rms(x_ref[...], g_ref[...]).astype(o_ref.dtype)


def _rmsnorm(x, g, tm):
    T, D = x.shape
    return pl.pallas_call(
        _rmsnorm_kernel,
        grid=(T // tm,),
        in_specs=[pl.BlockSpec((tm, D), lambda i: (i, 0)),
                  pl.BlockSpec((1, D), lambda i: (0, 0))],
        out_specs=pl.BlockSpec((tm, D), lambda i: (i, 0)),
        out_shape=jax.ShapeDtypeStruct((T, D), bf16),
        compiler_params=_cparams(1),
        name="rmsnorm",
    )(x, g)


def _inproj_kernel(h_ref, w_ref, o_ref):
    o_ref[...] = jnp.dot(h_ref[...], w_ref[...],
                         preferred_element_type=f32).astype(o_ref.dtype)


def _inproj(h, w, tm, tn):
    T, D = h.shape
    N = w.shape[1]
    return pl.pallas_call(
        _inproj_kernel,
        grid=(T // tm, N // tn),
        in_specs=[pl.BlockSpec((tm, D), lambda i, j: (i, 0)),
                  pl.BlockSpec((D, tn), lambda i, j: (0, j))],
        out_specs=pl.BlockSpec((tm, tn), lambda i, j: (i, j)),
        out_shape=jax.ShapeDtypeStruct((T, N), bf16),
        compiler_params=_cparams(2),
        name="inproj",
    )(h, w)


def _rope_kernel(pos_ref, inv_ref, sign_ref, cos_ref, sin_ref):
    ang = pos_ref[...].astype(f32) * inv_ref[...]
    cos_ref[...] = jnp.cos(ang)
    sin_ref[...] = jnp.sin(ang) * sign_ref[...]


def _rope_tables(pos, tm):
    T = pos.shape[0]
    half = MLA_ROPE // 2
    inv = 1.0 / (ROPE_THETA ** (jnp.arange(0, MLA_ROPE, 2, dtype=f32) / MLA_ROPE))
    zero = jnp.zeros((half,), f32)
    one = jnp.ones((half,), f32)
    inv_row = jnp.concatenate([inv, zero, inv, zero])[None, :]
    sign_row = jnp.concatenate([-one, zero, one, zero])[None, :]
    row = pl.BlockSpec((1, 128), lambda i: (0, 0))
    tab = pl.BlockSpec((tm, 128), lambda i: (i, 0))
    return pl.pallas_call(
        _rope_kernel,
        grid=(T // tm,),
        in_specs=[pl.BlockSpec((tm, 1), lambda i: (i, 0)), row, row],
        out_specs=[tab, tab],
        out_shape=[jax.ShapeDtypeStruct((T, 128), f32)] * 2,
        compiler_params=_cparams(1),
        name="rope_tables",
    )(pos, inv_row, sign_row)


def _mla_up_kernel(cq_ref, ckv_ref, kr_ref, cos_ref, sin_ref, gq_ref, gkv_ref,
                   wuq_ref, wuk_ref, wuvt_ref, q_ref, k_ref, vt_ref, *, qscale):
    cos = cos_ref[...]
    sin = sin_ref[...]

    def rope(x):
        return x * cos + pltpu.roll(x, 64, 1) * sin

    cqn = _rms(cq_ref[...].astype(f32), gq_ref[...]).astype(bf16)
    qf = jnp.dot(cqn, wuq_ref[...], preferred_element_type=f32) * qscale
    for h in range(MLA_HEADS):
        c0 = h * HEAD_PAD
        q_ref[0, h, 0, :, 0:128] = qf[:, c0:c0 + 128].astype(bf16)
        q_ref[0, h, 0, :, 128:256] = rope(qf[:, c0 + 128:c0 + 256]).astype(bf16)

    ckvn = _rms(ckv_ref[...].astype(f32), gkv_ref[...]).astype(bf16)
    kn = jnp.dot(ckvn, wuk_ref[...], preferred_element_type=f32)
    krr = rope(kr_ref[...].astype(f32)).astype(bf16)
    for h in range(MLA_HEADS):
        k_ref[0, h, 0, :, 0:128] = kn[:, h * 128:(h + 1) * 128].astype(bf16)
        k_ref[0, h, 0, :, 128:256] = krr

    vt = lax.dot_general(wuvt_ref[...], ckvn, (((1,), (1,)), ((), ())),
                         preferred_element_type=f32)
    ones = jnp.ones((VT_ROWS - MLA_V, vt.shape[1]), bf16)
    for h in range(MLA_HEADS):
        vt_ref[0, h, 0, 0:MLA_V, :] = vt[h * MLA_V:(h + 1) * MLA_V, :].astype(bf16)
        vt_ref[0, h, 0, MLA_V:VT_ROWS, :] = ones


def _mla_up(proj, cos_t, sin_t, gq, gkv, wuq, wuk, wuvt, B, S, blk):
    nb = S // blk
    H = MLA_HEADS

    def rows(b, j):
        return b * nb + j

    const = lambda b, j: (0, 0)
    qk_shape = jax.ShapeDtypeStruct((B, H, nb, blk, HEAD_PAD), bf16)
    qk_spec = pl.BlockSpec((1, H, 1, blk, HEAD_PAD), lambda b, j: (b, 0, j, 0, 0))
    qscale = (MLA_NOPE + MLA_ROPE) ** -0.5 * math.log2(math.e)
    return pl.pallas_call(
        functools.partial(_mla_up_kernel, qscale=qscale),
        grid=(B, nb),
        in_specs=[
            pl.BlockSpec((blk, 512), lambda b, j: (rows(b, j), P_CQ // 512)),
            pl.BlockSpec((blk, 512), lambda b, j: (rows(b, j), P_CKV // 512)),
            pl.BlockSpec((blk, 128), lambda b, j: (rows(b, j), P_KR // 128)),
            pl.BlockSpec((blk, 128), lambda b, j: (rows(b, j), 0)),
            pl.BlockSpec((blk, 128), lambda b, j: (rows(b, j), 0)),
            pl.BlockSpec((1, MLA_LORA), const),
            pl.BlockSpec((1, MLA_LORA), const),
            pl.BlockSpec(wuq.shape, const),
            pl.BlockSpec(wuk.shape, const),
            pl.BlockSpec(wuvt.shape, const),
        ],
        out_specs=[qk_spec, qk_spec,
                   pl.BlockSpec((1, H, 1, VT_ROWS, blk), lambda b, j: (b, 0, j, 0, 0))],
        out_shape=[qk_shape, qk_shape,
                   jax.ShapeDtypeStruct((B, H, nb, VT_ROWS, blk), bf16)],
        compiler_params=_cparams(2),
        name="mla_up",
    )(proj, proj, proj, cos_t, sin_t, gq, gkv, wuq, wuk, wuvt)


def _attn_kernel(q_ref, k_ref, vt_ref, z_ref, o_ref,
                 m_ref, r_ref, risk_ref, al0_ref, al1_ref, mt0_ref, mt1_ref,
                 acc_ref, accd_ref, md_ref, s0_ref, s1_ref, p0_ref, p1_ref, *, tq, tk):
    i = pl.program_id(2)
    kpq = tq // tk
    assert kpq % 2 == 0
    n_full = i * kpq
    nt = (((1,), (1,)), ((), ()))
    m_ref[...] = jnp.full(m_ref.shape, -1e30, f32)
    acc_ref[...] = jnp.zeros(acc_ref.shape, f32)

    def scores(j, cols):
        return lax.dot_general(k_ref[0, 0, j], q_ref[0, 0, 0, cols, :], nt,
                               preferred_element_type=f32)

    def put_scores(s, bufs, cols):
        s_ref, mt_ref, _ = bufs
        s_ref[:, cols] = s
        mt_ref[:, cols] = jnp.max(s, axis=0, keepdims=True)

    def softmax_pv(j, bufs, cols):
        s_ref, mt_ref, al_ref = bufs
        m_old = m_ref[:, cols]
        m_new = jnp.maximum(m_old, mt_ref[:, cols])
        al_ref[:, cols] = jnp.exp2(m_old - m_new)
        m_ref[:, cols] = m_new
        p = jnp.exp2(s_ref[:, cols] - m_new).astype(bf16)
        acc_ref[:, cols] = (al_ref[:, cols] * acc_ref[:, cols]
                            + jnp.dot(vt_ref[0, 0, j], p, preferred_element_type=f32))

    everything = slice(0, tq)
    buf0, buf1 = (s0_ref, mt0_ref, al0_ref), (s1_ref, mt1_ref, al1_ref)

    def put_diag_scores(kt):
        cols = slice(kt * tk, tq)
        s = scores(n_full + kt, cols)
        kc = lax.broadcasted_iota(jnp.int32, s.shape, 0) // CHUNK
        qc = lax.broadcasted_iota(jnp.int32, s.shape, 1) // CHUNK
        put_scores(jnp.where(kc <= qc, s, -1e30), (buf0, buf1)[kt % 2], cols)

    put_diag_scores(0)
    for kt in range(kpq):
        if kt + 1 < kpq:
            put_diag_scores(kt + 1)
        softmax_pv(n_full + kt, (buf0, buf1)[kt % 2], slice(kt * tk, tq))

    accd_ref[...] = acc_ref[...]
    md_ref[...] = m_ref[...]
    r_ref[...] = m_ref[...]
    risk_ref[...] = jnp.zeros(risk_ref.shape, f32)
    p1_ref[...] = jnp.zeros(p1_ref.shape, bf16)
    al1_ref[...] = jnp.ones(al1_ref.shape, f32)
    fast = ((p0_ref, al0_ref), (p1_ref, al1_ref))

    def fast_pv(j, slot):
        p_ref, al_ref = fast[slot]
        acc_ref[...] = al_ref[...] * acc_ref[...] + jnp.dot(vt_ref[0, 0, j], p_ref[...],
                                                             preferred_element_type=f32)

    def fast_step(j, slot):
        p_ref, al_ref = fast[slot]
        ref = m_ref[...]
        s = scores(j, everything)
        mt = jnp.max(s, axis=0, keepdims=True)
        p_ref[...] = jnp.exp2(s - ref).astype(bf16)
        al_ref[...] = jnp.exp2(r_ref[...] - ref)
        r_ref[...] = ref
        risk_ref[...] = jnp.maximum(risk_ref[...], mt - ref)
        m_ref[...] = jnp.maximum(ref, mt)
        fast_pv(jnp.maximum(j - 1, 0), 1 - slot)

    def fast_body(t, carry):
        for u in range(kpq):
            fast_step(kpq * t + u, u % 2)
        return carry

    lax.fori_loop(0, i, fast_body, 0)
    fast_pv(jnp.maximum(n_full - 1, 0), 1)

    @pl.when(jnp.max(risk_ref[...]) > FAST_MAX_JUMP)
    def _():
        acc_ref[...] = accd_ref[...]
        m_ref[...] = md_ref[...]
        put_scores(scores(0, everything), buf0, everything)

        def body(t, carry):
            for u in range(kpq):
                j = kpq * t + u
                put_scores(scores(j + 1, everything), (buf0, buf1)[(u + 1) % 2], everything)
                softmax_pv(j, (buf0, buf1)[u % 2], everything)
            return carry

        lax.fori_loop(0, i, body, 0)

    o = acc_ref[0:MLA_V, :] / acc_ref[MLA_V:MLA_V + 1, :]
    z = z_ref[...].astype(f32)
    o_ref[...] = (o.T * _silu(z)).astype(o_ref.dtype)


def _attention(q, k, vt, proj, B, S, tq, tk):
    H = MLA_HEADS
    nq, nk = S // tq, S // tk
    T = B * S
    q = q.reshape(B, H, nq, tq, HEAD_PAD)
    row = pltpu.VMEM((1, tq), f32)
    acc = pltpu.VMEM((VT_ROWS, tq), f32)
    return pl.pallas_call(
        functools.partial(_attn_kernel, tq=tq, tk=tk),
        grid=(B, H, nq),
        in_specs=[
            pl.BlockSpec((1, 1, 1, tq, HEAD_PAD), lambda b, h, i: (b, h, i, 0, 0)),
            pl.BlockSpec((1, 1, nk, tk, HEAD_PAD), lambda b, h, i: (b, h, 0, 0, 0)),
            pl.BlockSpec((1, 1, nk, VT_ROWS, tk), lambda b, h, i: (b, h, 0, 0, 0)),
            pl.BlockSpec((tq, MLA_V), lambda b, h, i: (b * nq + i, P_ZA // MLA_V + h)),
        ],
        out_specs=pl.BlockSpec((tq, MLA_V), lambda b, h, i: (b * nq + i, h)),
        out_shape=jax.ShapeDtypeStruct((T, BRANCH_WIDTH), bf16),
        scratch_shapes=[row] * 7 + [acc, acc, row,
                                    pltpu.VMEM((tk, tq), f32), pltpu.VMEM((tk, tq), f32),
                                    pltpu.VMEM((tk, tq), bf16), pltpu.VMEM((tk, tq), bf16)],
        compiler_params=_cparams(3),
        name="mla_attention",
    )(q, k, vt, proj)


def _pool_kernel(u_ref, z_ref, w_ref, sc_ref, o_ref, ext_ref, *, tm):
    j = pl.program_id(1)
    H = POOL_HALO

    @pl.when(j == 0)
    def _():
        ext_ref[0:H, :] = jnp.zeros((H, ext_ref.shape[1]), f32)

    @pl.when(j > 0)
    def _():
        ext_ref[0:H, :] = ext_ref[tm:tm + H, :]

    u = u_ref[...].astype(f32)
    ext_ref[H:H + tm, :] = u
    t = j * tm + lax.broadcasted_iota(jnp.int32, (tm, 1), 0)
    for g, w in enumerate(POOL_WINDOWS):
        c0, c1 = g * POOL_GROUP_W, (g + 1) * POOL_GROUP_W
        ug = u[:, c0:c1]
        acc = ug
        for kk in range(1, w):
            acc = acc + ext_ref[H - kk:H - kk + tm, c0:c1]
        cnt = jnp.minimum(t + 1, w).astype(f32)
        m = acc / cnt - ug
        y = jnp.dot(m.astype(bf16), w_ref[g], preferred_element_type=f32) * sc_ref[:, c0:c1]
        o_ref[:, c0:c1] = (y * _silu(z_ref[:, c0:c1].astype(f32))).astype(o_ref.dtype)


def _pool(proj, pool_w, pool_scale, B, S, tm):
    nb = S // tm
    T = B * S
    W = BRANCH_WIDTH
    return pl.pallas_call(
        functools.partial(_pool_kernel, tm=tm),
        grid=(B, nb),
        in_specs=[
            pl.BlockSpec((tm, W), lambda b, j: (b * nb + j, P_UB // W)),
            pl.BlockSpec((tm, W), lambda b, j: (b * nb + j, P_ZB // W)),
            pl.BlockSpec(pool_w.shape, lambda b, j: (0, 0, 0)),
            pl.BlockSpec((1, W), lambda b, j: (0, 0)),
        ],
        out_specs=pl.BlockSpec((tm, W), lambda b, j: (b * nb + j, 0)),
        out_shape=jax.ShapeDtypeStruct((T, W), bf16),
        scratch_shapes=[pltpu.VMEM((tm + POOL_HALO, W), f32)],
        compiler_params=_cparams(2),
        name="pool",
    )(proj, proj, pool_w, pool_scale)


def _gla_kernel(q_ref, k_ref, v_ref, glr_ref, z_ref, wg_ref, bg_ref, ng_ref, o_ref,
                st_ref, *, tg):
    j = pl.program_id(1)

    @pl.when(j == 0)
    def _():
        st_ref[...] = jnp.zeros(st_ref.shape, f32)

    x = jnp.dot(glr_ref[...], wg_ref[...], preferred_element_type=f32) + bg_ref[...]
    log_a = -(jnp.maximum(-x, 0.0) + jnp.log(1.0 + jnp.exp(-jnp.abs(x)))) / GLA_GATE_TAU

    r = lax.broadcasted_iota(jnp.int32, (tg, tg), 0)
    c = lax.broadcasted_iota(jnp.int32, (tg, tg), 1)
    tri = ((r // CHUNK == c // CHUNK) & (c <= r)).astype(bf16)
    hi = log_a.astype(bf16)
    rem = log_a - hi.astype(f32)
    mid = rem.astype(bf16)
    lo = (rem - mid.astype(f32)).astype(bf16)
    bcum = (jnp.dot(tri, hi, preferred_element_type=f32)
            + jnp.dot(tri, mid, preferred_element_type=f32)
            + jnp.dot(tri, lo, preferred_element_type=f32))

    n_chunks = tg // CHUNK
    q = q_ref[...].astype(f32) * (GLA_DK ** -0.5)
    k = k_ref[...].astype(f32)
    qd = (q * jnp.exp(bcum)).astype(bf16)
    kd = (k * jnp.exp(-bcum)).astype(bf16)
    b_last = [bcum[(g + 1) * CHUNK - 1:(g + 1) * CHUNK, :] for g in range(n_chunks)]
    b_last_rows = jnp.concatenate([jnp.broadcast_to(b, (CHUNK, b.shape[1])) for b in b_last], axis=0)
    ke = (k * jnp.exp(b_last_rows - bcum)).astype(bf16)
    decay = [jnp.exp(b) for b in b_last]
    same_chunk_causal = (r // CHUNK == c // CHUNK) & (c <= r)
    nt = (((1,), (1,)), ((), ()))

    def block_diag(xh):
        blocks = []
        for g in range(n_chunks):
            parts = []
            if g > 0:
                parts.append(jnp.zeros((g * CHUNK, xh.shape[1]), xh.dtype))
            parts.append(xh[g * CHUNK:(g + 1) * CHUNK, :])
            if g < n_chunks - 1:
                parts.append(jnp.zeros((tg - (g + 1) * CHUNK, xh.shape[1]), xh.dtype))
            blocks.append(jnp.concatenate(parts, axis=0))
        return jnp.concatenate(blocks, axis=1)

    for h in range(GLA_HEADS):
        k0, k1 = h * GLA_DK, (h + 1) * GLA_DK
        v0, v1 = h * GLA_DV, (h + 1) * GLA_DV
        v = v_ref[:, v0:v1]
        att = lax.dot_general(qd[:, k0:k1], kd[:, k0:k1], nt, preferred_element_type=f32)
        att = jnp.where(same_chunk_causal, att, 0.0).astype(bf16)
        ut_all = lax.dot_general(v, block_diag(ke[:, k0:k1]), (((0,), (0,)), ((), ())),
                                 preferred_element_type=f32)
        st = st_ref[h]
        st_in = []
        for g in range(n_chunks):
            st_in.append(st.astype(bf16))
            st = st * decay[g][:, k0:k1] + ut_all[:, g * GLA_DK:(g + 1) * GLA_DK]
        st_ref[h] = st
        o = (jnp.dot(att, v, preferred_element_type=f32)
             + lax.dot_general(block_diag(qd[:, k0:k1]), jnp.concatenate(st_in, axis=1), nt,
                               preferred_element_type=f32))
        on = _rms(o, ng_ref[:, v0:v1])
        o_ref[:, v0:v1] = (on * _silu(z_ref[:, v0:v1].astype(f32))).astype(o_ref.dtype)


def _gla(proj, wg, bg, ng, B, S, tg):
    nb = S // tg
    T = B * S
    W = BRANCH_WIDTH
    QK = GLA_HEADS * GLA_DK
    rows = lambda b, j: b * nb + j
    return pl.pallas_call(
        functools.partial(_gla_kernel, tg=tg),
        grid=(B, nb),
        in_specs=[
            pl.BlockSpec((tg, QK), lambda b, j: (rows(b, j), P_QC // QK)),
            pl.BlockSpec((tg, QK), lambda b, j: (rows(b, j), P_KC // QK)),
            pl.BlockSpec((tg, W), lambda b, j: (rows(b, j), P_VC // W)),
            pl.BlockSpec((tg, 128), lambda b, j: (rows(b, j), P_GLR // 128)),
            pl.BlockSpec((tg, W), lambda b, j: (rows(b, j), P_ZC // W)),
            pl.BlockSpec(wg.shape, lambda b, j: (0, 0)),
            pl.BlockSpec((1, QK), lambda b, j: (0, 0)),
            pl.BlockSpec((1, W), lambda b, j: (0, 0)),
        ],
        out_specs=pl.BlockSpec((tg, W), lambda b, j: (rows(b, j), 0)),
        out_shape=jax.ShapeDtypeStruct((T, W), bf16),
        scratch_shapes=[pltpu.VMEM((GLA_HEADS, GLA_DV, GLA_DK), f32)],
        compiler_params=_cparams(2),
        name="gla",
    )(proj, proj, proj, proj, proj, wg, bg, ng)


def _merge_kernel(ya_ref, yb_ref, yc_ref, g0_ref, g1_ref, g2_ref, bm_ref, wb_ref, wo_ref,
                  x_ref, ng_ref, *out_refs, final):
    merged = None
    for n, (y_ref, g_ref) in enumerate(((ya_ref, g0_ref), (yb_ref, g1_ref), (yc_ref, g2_ref))):
        p = jnp.dot(y_ref[...], wb_ref[n], preferred_element_type=f32)
        gate = 1.0 / (1.0 + jnp.exp(-(g_ref[...].astype(f32) + bm_ref[n:n + 1, :])))
        merged = gate * p if merged is None else merged + gate * p
    xn = x_ref[...] + jnp.dot(merged.astype(bf16), wo_ref[...], preferred_element_type=f32)
    hn = _rms(xn, ng_ref[...])
    if final:
        out_refs[0][...] = hn
    else:
        out_refs[0][...] = xn
        out_refs[1][...] = hn.astype(bf16)


def _merge(ya, yb, yc, proj, bm, wb, wo, x, ng, tm, final):
    T, D = x.shape
    W = BRANCH_WIDTH
    row = lambda i: (i, 0)
    ytile = pl.BlockSpec((tm, W), row)
    xtile = pl.BlockSpec((tm, D), row)
    resident = dict(pipeline_mode=pl.Buffered(1))
    if final:
        out_specs = [xtile]
        out_shape = [jax.ShapeDtypeStruct((T, D), f32)]
    else:
        out_specs = [xtile, xtile]
        out_shape = [jax.ShapeDtypeStruct((T, D), f32), jax.ShapeDtypeStruct((T, D), bf16)]
    return pl.pallas_call(
        functools.partial(_merge_kernel, final=final),
        grid=(T // tm,),
        in_specs=[
            ytile, ytile, ytile,
            pl.BlockSpec((tm, D), lambda i: (i, P_GATES // D + 0)),
            pl.BlockSpec((tm, D), lambda i: (i, P_GATES // D + 1)),
            pl.BlockSpec((tm, D), lambda i: (i, P_GATES // D + 2)),
            pl.BlockSpec((N_BRANCH, D), lambda i: (0, 0)),
            pl.BlockSpec(wb.shape, lambda i: (0, 0, 0), **resident),
            pl.BlockSpec(wo.shape, lambda i: (0, 0), **resident),
            xtile,
            pl.BlockSpec((1, D), lambda i: (0, 0)),
        ],
        out_specs=out_specs,
        out_shape=out_shape,
        compiler_params=_cparams(1),
        name="merge_final" if final else "merge",
    )(ya, yb, yc, proj, proj, proj, bm, wb, wo, x, ng)


def _permute_w_in(w):
    D = w.shape[0]
    half = MLA_ROPE // 2
    w = w.astype(bf16)
    sl = lambda o, n: w[:, o:o + n]
    z = lambda n: jnp.zeros((D, n), w.dtype)
    parts = [
        sl(_O_GATES, N_BRANCH * D_MODEL), sl(_O_ZA, 1024), sl(_O_UB, 1024), sl(_O_ZB, 1024),
        sl(_O_VC, 1024), sl(_O_ZC, 1024), sl(_O_CQ, 512), sl(_O_CKV, 512),
        sl(_O_QC, 512), sl(_O_KC, 512),
        sl(_O_KR, half), z(half), sl(_O_KR + half, half), z(half),
        sl(_O_GLR, GLA_GATE_RANK), z(128 - GLA_GATE_RANK),
    ]
    used = P_GLR + 128
    parts.append(z(P_TOTAL - used))
    return jnp.concatenate(parts, axis=1)


def _permute_w_uq(w):
    half = MLA_ROPE // 2
    w = w.reshape(MLA_LORA, MLA_HEADS, MLA_NOPE + MLA_ROPE)
    z = jnp.zeros((MLA_LORA, MLA_HEADS, half), w.dtype)
    out = jnp.concatenate([w[..., :MLA_NOPE], w[..., MLA_NOPE:MLA_NOPE + half], z,
                           w[..., MLA_NOPE + half:], z], axis=-1)
    return out.reshape(MLA_LORA, MLA_HEADS * HEAD_PAD).astype(bf16)


def _split_w_ukv(w):
    w = w.reshape(MLA_LORA, MLA_HEADS, MLA_NOPE + MLA_V)
    wuk = w[..., :MLA_NOPE].reshape(MLA_LORA, MLA_HEADS * MLA_NOPE).astype(bf16)
    wuvt = w[..., MLA_NOPE:].reshape(MLA_LORA, MLA_HEADS * MLA_V).T.astype(bf16)
    return wuk, wuvt


def _pick(n, pref):
    return pref if n % pref == 0 else n


def kernel(x, positions, norm_g, w_in, b_merge, mla_q_norm, mla_kv_norm, mla_w_uq, mla_w_ukv,
           pool_w, pool_scale, gla_w_gate, gla_b_gate, gla_norm, w_branch, w_out, final_norm):
    B, S, D = x.shape
    assert D == D_MODEL and S % 2048 == 0
    T = B * S
    depth = norm_g.shape[0]
    blk = 512
    tq_attn = 2048
    tm_proj = _pick(T, 1024)
    tn_proj = 1536
    tm_pool = 512
    tg = 256
    tm_merge = 256

    xf = x.reshape(T, D)
    cos_t, sin_t = _rope_tables(positions.reshape(T, 1), 512)
    h = _rmsnorm(xf, norm_g[0][None, :], 512)
    out = None
    for l in range(depth):
        w_in_p = _permute_w_in(w_in[l])
        wuq = _permute_w_uq(mla_w_uq[l])
        wuk, wuvt = _split_w_ukv(mla_w_ukv[l])
        wg = jnp.zeros((128, GLA_HEADS * GLA_DK), bf16).at[:GLA_GATE_RANK].set(
            gla_w_gate[l].astype(bf16))

        proj = _inproj(h, w_in_p, tm_proj, tn_proj)
        q, k, vt = _mla_up(proj, cos_t, sin_t, mla_q_norm[l][None, :], mla_kv_norm[l][None, :],
                           wuq, wuk, wuvt, B, S, blk)
        ya = _attention(q, k, vt, proj, B, S, tq_attn, blk)
        yb = _pool(proj, pool_w[l].astype(bf16), pool_scale[l][None, :], B, S, tm_pool)
        yc = _gla(proj, wg, gla_b_gate[l][None, :], gla_norm[l][None, :], B, S, tg)

        final = l == depth - 1
        next_g = final_norm if final else norm_g[l + 1]
        res = _merge(ya, yb, yc, proj, b_merge[l], w_branch[l].astype(bf16),
                     w_out[l].astype(bf16), xf, next_g[None, :], tm_merge, final)
        if final:
            out = res[0]
        else:
            xf, h = res
    return out.reshape(B, S, D)
```

```python
import functools
import math

import jax
import jax.numpy as jnp
import numpy as np
from jax import lax
from jax.experimental import pallas as pl
from jax.experimental.pallas import tpu as pltpu

f32 = jnp.float32
bf16 = jnp.bfloat16

D_MODEL = 2048
CHUNK = 64
EPS = 1e-6

MLA_HEADS = 8
MLA_NOPE = 128
MLA_ROPE = 64
MLA_V = 128
VT_ROWS = MLA_V + 16
FAST_MAX_JUMP = 24.0
MLA_LORA = 512
ROPE_THETA = 10000.0
HEAD_PAD = 256

POOL_WINDOWS = (2, 4, 8, 16)
POOL_GROUP_W = 256
POOL_HALO = 16

GLA_HEADS = 4
GLA_DK = 128
GLA_DV = 256
GLA_GATE_RANK = 16
GLA_GATE_TAU = 16.0

N_BRANCH = 3
BRANCH_WIDTH = 1024

_O_CQ, _O_CKV, _O_KR, _O_ZA, _O_UB, _O_ZB = 0, 512, 1024, 1088, 2112, 3136
_O_QC, _O_KC, _O_VC, _O_GLR, _O_ZC, _O_GATES = 4160, 4672, 5184, 6208, 6224, 7248

P_GATES, P_ZA, P_UB, P_ZB, P_VC, P_ZC = 0, 6144, 7168, 8192, 9216, 10240
P_CQ, P_CKV, P_QC, P_KC, P_KR, P_GLR = 11264, 11776, 12288, 12800, 13312, 13440
P_TOTAL = 13824

VMEM_LIMIT = 56 * 1024 * 1024


def _cparams(n_axes, flags=None):
    return pltpu.CompilerParams(
        dimension_semantics=("arbitrary",) * n_axes, vmem_limit_bytes=VMEM_LIMIT, flags=flags)


def _silu(z):
    return z * (1.0 / (1.0 + jnp.exp(-z)))


def _rms(x, g):
    return x * lax.rsqrt(jnp.mean(x * x, axis=-1, keepdims=True) + EPS) * g


def _rmsnorm_kernel(x_ref, g_ref, o_ref):
    o_ref[...] = _rms(x_ref[...], g_ref[...]).astype(o_ref.dtype)


def _rmsnorm(x, g, tm):
    T, D = x.shape
    return pl.pallas_call(
        _rmsnorm_kernel,
        grid=(T // tm,),
        in_specs=[pl.BlockSpec((tm, D), lambda i: (i, 0)),
                  pl.BlockSpec((1, D), lambda i: (0, 0))],
        out_specs=pl.BlockSpec((tm, D), lambda i: (i, 0)),
        out_shape=jax.ShapeDtypeStruct((T, D), bf16),
        compiler_params=_cparams(1),
        name="rmsnorm",
    )(x, g)


def _inproj_kernel(h_ref, w_ref, o_ref):
    o_ref[...] = jnp.dot(h_ref[...], w_ref[...],
                         preferred_element_type=f32).astype(o_ref.dtype)


def _inproj(h, w, layer, tm, tn):
    T, D = h.shape
    N = w.shape[2]
    return pl.pallas_call(
        _inproj_kernel,
        grid=(T // tm, N // tn),
        in_specs=[pl.BlockSpec((tm, D), lambda i, j: (i, 0)),
                  pl.BlockSpec((None, D, tn), lambda i, j: (layer, 0, j))],
        out_specs=pl.BlockSpec((tm, tn), lambda i, j: (i, j)),
        out_shape=jax.ShapeDtypeStruct((T, N), bf16),
        compiler_params=_cparams(2),
        name="inproj",
    )(h, w)


def _rope_kernel(pos_ref, inv_ref, sign_ref, cos_ref, sin_ref):
    ang = pos_ref[...].astype(f32) * inv_ref[...]
    cos_ref[...] = jnp.cos(ang)
    sin_ref[...] = jnp.sin(ang) * sign_ref[...]


def _rope_tables(pos, tm):
    T = pos.shape[0]
    half = MLA_ROPE // 2
    inv = 1.0 / (ROPE_THETA ** (jnp.arange(0, MLA_ROPE, 2, dtype=f32) / MLA_ROPE))
    zero = jnp.zeros((half,), f32)
    one = jnp.ones((half,), f32)
    inv_row = jnp.concatenate([inv, zero, inv, zero])[None, :]
    sign_row = jnp.concatenate([-one, zero, one, zero])[None, :]
    row = pl.BlockSpec((1, 128), lambda i: (0, 0))
    tab = pl.BlockSpec((tm, 128), lambda i: (i, 0))
    return pl.pallas_call(
        _rope_kernel,
        grid=(T // tm,),
        in_specs=[pl.BlockSpec((tm, 1), lambda i: (i, 0)), row, row],
        out_specs=[tab, tab],
        out_shape=[jax.ShapeDtypeStruct((T, 128), f32)] * 2,
        compiler_params=_cparams(1),
        name="rope_tables",
    )(pos, inv_row, sign_row)


def _mla_up_kernel(cq_ref, ckv_ref, kr_ref, cos_ref, sin_ref, gq_ref, gkv_ref,
                   wuq_ref, wuk_ref, wuvt_ref, q_ref, k_ref, vt_ref, *, qscale):
    cos = cos_ref[...]
    sin = sin_ref[...]

    def rope(x):
        return x * cos + pltpu.roll(x, 64, 1) * sin

    cqn = _rms(cq_ref[...].astype(f32), gq_ref[...]).astype(bf16)
    qf = jnp.dot(cqn, wuq_ref[...], preferred_element_type=f32) * qscale
    for h in range(MLA_HEADS):
        c0 = h * HEAD_PAD
        q_ref[0, h, 0, :, 0:128] = qf[:, c0:c0 + 128].astype(bf16)
        q_ref[0, h, 0, :, 128:256] = rope(qf[:, c0 + 128:c0 + 256]).astype(bf16)

    ckvn = _rms(ckv_ref[...].astype(f32), gkv_ref[...]).astype(bf16)
    kn = jnp.dot(ckvn, wuk_ref[...], preferred_element_type=f32)
    krr = rope(kr_ref[...].astype(f32)).astype(bf16)
    for h in range(MLA_HEADS):
        k_ref[0, h, 0, :, 0:128] = kn[:, h * 128:(h + 1) * 128].astype(bf16)
        k_ref[0, h, 0, :, 128:256] = krr

    vt = lax.dot_general(wuvt_ref[...], ckvn, (((1,), (1,)), ((), ())),
                         preferred_element_type=f32)
    ones = jnp.ones((VT_ROWS - MLA_V, vt.shape[1]), bf16)
    for h in range(MLA_HEADS):
        vt_ref[0, h, 0, 0:MLA_V, :] = vt[h * MLA_V:(h + 1) * MLA_V, :].astype(bf16)
        vt_ref[0, h, 0, MLA_V:VT_ROWS, :] = ones


def _mla_up(proj, cos_t, sin_t, gq, gkv, wuq, wuk, wuvt, B, S, blk):
    nb = S // blk
    H = MLA_HEADS

    def rows(b, j):
        return b * nb + j

    const = lambda b, j: (0, 0)
    qk_shape = jax.ShapeDtypeStruct((B, H, nb, blk, HEAD_PAD), bf16)
    qk_spec = pl.BlockSpec((1, H, 1, blk, HEAD_PAD), lambda b, j: (b, 0, j, 0, 0))
    qscale = (MLA_NOPE + MLA_ROPE) ** -0.5 * math.log2(math.e)
    return pl.pallas_call(
        functools.partial(_mla_up_kernel, qscale=qscale),
        grid=(B, nb),
        in_specs=[
            pl.BlockSpec((blk, 512), lambda b, j: (rows(b, j), P_CQ // 512)),
            pl.BlockSpec((blk, 512), lambda b, j: (rows(b, j), P_CKV // 512)),
            pl.BlockSpec((blk, 128), lambda b, j: (rows(b, j), P_KR // 128)),
            pl.BlockSpec((blk, 128), lambda b, j: (rows(b, j), 0)),
            pl.BlockSpec((blk, 128), lambda b, j: (rows(b, j), 0)),
            pl.BlockSpec((1, MLA_LORA), const),
            pl.BlockSpec((1, MLA_LORA), const),
            pl.BlockSpec(wuq.shape, const),
            pl.BlockSpec(wuk.shape, const),
            pl.BlockSpec(wuvt.shape, const),
        ],
        out_specs=[qk_spec, qk_spec,
                   pl.BlockSpec((1, H, 1, VT_ROWS, blk), lambda b, j: (b, 0, j, 0, 0))],
        out_shape=[qk_shape, qk_shape,
                   jax.ShapeDtypeStruct((B, H, nb, VT_ROWS, blk), bf16)],
        compiler_params=_cparams(2),
        name="mla_up",
    )(proj, proj, proj, cos_t, sin_t, gq, gkv, wuq, wuk, wuvt)


def _attn_kernel(q_ref, k_ref, vt_ref, z_ref, o_ref,
                 m_ref, r_ref, risk_ref, al0_ref, al1_ref, mt0_ref, mt1_ref,
                 acc_ref, accd_ref, md_ref, s0_ref, s1_ref, p0_ref, p1_ref, *, tq, tk):
    i = pl.program_id(2)
    kpq = tq // tk
    assert kpq % 2 == 0
    n_full = i * kpq
    nt = (((1,), (1,)), ((), ()))
    m_ref[...] = jnp.full(m_ref.shape, -1e30, f32)
    acc_ref[...] = jnp.zeros(acc_ref.shape, f32)

    def scores(j, cols):
        return lax.dot_general(k_ref[0, 0, j], q_ref[0, 0, 0, cols, :], nt,
                               preferred_element_type=f32)

    def put_scores(s, bufs, cols):
        s_ref, mt_ref, _ = bufs
        s_ref[:, cols] = s
        mt_ref[:, cols] = jnp.max(s, axis=0, keepdims=True)

    def softmax_pv(j, bufs, cols):
        s_ref, mt_ref, al_ref = bufs
        m_old = m_ref[:, cols]
        m_new = jnp.maximum(m_old, mt_ref[:, cols])
        al_ref[:, cols] = jnp.exp2(m_old - m_new)
        m_ref[:, cols] = m_new
        p = jnp.exp2(s_ref[:, cols] - m_new).astype(bf16)
        acc_ref[:, cols] = (al_ref[:, cols] * acc_ref[:, cols]
                            + jnp.dot(vt_ref[0, 0, j], p, preferred_element_type=f32))

    everything = slice(0, tq)
    buf0, buf1 = (s0_ref, mt0_ref, al0_ref), (s1_ref, mt1_ref, al1_ref)

    def put_diag_scores(kt):
        cols = slice(kt * tk, tq)
        s = scores(n_full + kt, cols)
        kc = lax.broadcasted_iota(jnp.int32, s.shape, 0) // CHUNK
        qc = lax.broadcasted_iota(jnp.int32, s.shape, 1) // CHUNK
        put_scores(jnp.where(kc <= qc, s, -1e30), (buf0, buf1)[kt % 2], cols)

    put_diag_scores(0)
    for kt in range(kpq):
        if kt + 1 < kpq:
            put_diag_scores(kt + 1)
        softmax_pv(n_full + kt, (buf0, buf1)[kt % 2], slice(kt * tk, tq))

    accd_ref[...] = acc_ref[...]
    md_ref[...] = m_ref[...]
    r_ref[...] = m_ref[...]
    risk_ref[...] = jnp.zeros(risk_ref.shape, f32)
    p1_ref[...] = jnp.zeros(p1_ref.shape, bf16)
    al1_ref[...] = jnp.ones(al1_ref.shape, f32)
    fast = ((p0_ref, al0_ref), (p1_ref, al1_ref))

    def fast_pv(j, slot):
        p_ref, al_ref = fast[slot]
        acc_ref[...] = al_ref[...] * acc_ref[...] + jnp.dot(vt_ref[0, 0, j], p_ref[...],
                                                             preferred_element_type=f32)

    def fast_step(j, slot):
        p_ref, al_ref = fast[slot]
        ref = m_ref[...]
        s = scores(j, everything)
        mt = jnp.max(s, axis=0, keepdims=True)
        p_ref[...] = jnp.exp2(s - ref).astype(bf16)
        al_ref[...] = jnp.exp2(r_ref[...] - ref)
        r_ref[...] = ref
        risk_ref[...] = jnp.maximum(risk_ref[...], mt - ref)
        m_ref[...] = jnp.maximum(ref, mt)
        fast_pv(jnp.maximum(j - 1, 0), 1 - slot)

    def fast_body(t, carry):
        for u in range(kpq):
            fast_step(kpq * t + u, u % 2)
        return carry

    lax.fori_loop(0, i, fast_body, 0)
    fast_pv(jnp.maximum(n_full - 1, 0), 1)

    @pl.when(jnp.max(risk_ref[...]) > FAST_MAX_JUMP)
    def _():
        acc_ref[...] = accd_ref[...]
        m_ref[...] = md_ref[...]
        put_scores(scores(0, everything), buf0, everything)

        def body(t, carry):
            for u in range(kpq):
                j = kpq * t + u
                put_scores(scores(j + 1, everything), (buf0, buf1)[(u + 1) % 2], everything)
                softmax_pv(j, (buf0, buf1)[u % 2], everything)
            return carry

        lax.fori_loop(0, i, body, 0)

    o = acc_ref[0:MLA_V, :] / acc_ref[MLA_V:MLA_V + 1, :]
    z = z_ref[...].astype(f32)
    o_ref[...] = (o.T * _silu(z)).astype(o_ref.dtype)


def _attention(q, k, vt, proj, B, S, tq, tk):
    H = MLA_HEADS
    nq, nk = S // tq, S // tk
    T = B * S
    q = q.reshape(B, H, nq, tq, HEAD_PAD)
    row = pltpu.VMEM((1, tq), f32)
    acc = pltpu.VMEM((VT_ROWS, tq), f32)
    return pl.pallas_call(
        functools.partial(_attn_kernel, tq=tq, tk=tk),
        grid=(B, H, nq),
        in_specs=[
            pl.BlockSpec((1, 1, 1, tq, HEAD_PAD), lambda b, h, i: (b, h, i, 0, 0)),
            pl.BlockSpec((1, 1, nk, tk, HEAD_PAD), lambda b, h, i: (b, h, 0, 0, 0)),
            pl.BlockSpec((1, 1, nk, VT_ROWS, tk), lambda b, h, i: (b, h, 0, 0, 0)),
            pl.BlockSpec((tq, MLA_V), lambda b, h, i: (b * nq + i, P_ZA // MLA_V + h)),
        ],
        out_specs=pl.BlockSpec((tq, MLA_V), lambda b, h, i: (b * nq + i, h)),
        out_shape=jax.ShapeDtypeStruct((T, BRANCH_WIDTH), bf16),
        scratch_shapes=[row] * 7 + [acc, acc, row,
                                    pltpu.VMEM((tk, tq), f32), pltpu.VMEM((tk, tq), f32),
                                    pltpu.VMEM((tk, tq), bf16), pltpu.VMEM((tk, tq), bf16)],
        compiler_params=_cparams(3),
        name="mla_attention",
    )(q, k, vt, proj)


def _pool_kernel(u_ref, z_ref, bm_ref, bh_ref, w_ref, sc_ref, o_ref, halo_ref, *, tm):
    j = pl.program_id(1)
    H = POOL_HALO

    @pl.when(j == 0)
    def _():
        halo_ref[...] = jnp.zeros(halo_ref.shape, bf16)

    u = u_ref[...]
    halo = halo_ref[...]
    t = j * tm + lax.broadcasted_iota(jnp.int32, (tm, 1), 0)
    groups = [(g, w, slice(g * POOL_GROUP_W, (g + 1) * POOL_GROUP_W)) for g, w in enumerate(POOL_WINDOWS)]
    sums = [jnp.dot(bm_ref[g], u[:, cs], preferred_element_type=f32)
            + jnp.dot(bh_ref[g], halo[:, cs], preferred_element_type=f32) for g, w, cs in groups]
    means = [(sums[g] / jnp.minimum(t + 1, w).astype(f32) - u[:, cs].astype(f32)).astype(bf16)
             for g, w, cs in groups]
    ys = [jnp.dot(means[g], w_ref[g], preferred_element_type=f32) for g, w, cs in groups]
    for g, w, cs in groups:
        o_ref[:, cs] = (ys[g] * sc_ref[:, cs] * _silu(z_ref[:, cs].astype(f32))).astype(o_ref.dtype)
    halo_ref[...] = u[tm - H:tm, :]


def _pool_bands(tm):
    t = np.arange(tm)[:, None]
    main = np.stack([((t - np.arange(tm)[None, :] >= 0) & (t - np.arange(tm)[None, :] < w))
                     for w in POOL_WINDOWS])
    halo = np.stack([(t + POOL_HALO - np.arange(POOL_HALO)[None, :] < w) for w in POOL_WINDOWS])
    return jnp.asarray(main, bf16), jnp.asarray(halo, bf16)


def _pool(proj, pool_w, pool_scale, B, S, tm):
    nb = S // tm
    T = B * S
    W = BRANCH_WIDTH
    band_main, band_halo = _pool_bands(tm)
    const3 = lambda b, j: (0, 0, 0)
    return pl.pallas_call(
        functools.partial(_pool_kernel, tm=tm),
        grid=(B, nb),
        in_specs=[
            pl.BlockSpec((tm, W), lambda b, j: (b * nb + j, P_UB // W)),
            pl.BlockSpec((tm, W), lambda b, j: (b * nb + j, P_ZB // W)),
            pl.BlockSpec(band_main.shape, const3),
            pl.BlockSpec(band_halo.shape, const3),
            pl.BlockSpec(pool_w.shape, const3),
            pl.BlockSpec((1, W), lambda b, j: (0, 0)),
        ],
        out_specs=pl.BlockSpec((tm, W), lambda b, j: (b * nb + j, 0)),
        out_shape=jax.ShapeDtypeStruct((T, W), bf16),
        scratch_shapes=[pltpu.VMEM((POOL_HALO, W), bf16)],
        compiler_params=_cparams(2),
        name="pool",
    )(proj, proj, band_main, band_halo, pool_w, pool_scale)


def _gla_kernel(q_ref, k_ref, v_ref, glr_ref, z_ref, wg_ref, bg_ref, ng_ref, o_ref,
                st_ref, *, tg):
    j = pl.program_id(1)

    @pl.when(j == 0)
    def _():
        st_ref[...] = jnp.zeros(st_ref.shape, f32)

    x = jnp.dot(glr_ref[...], wg_ref[...], preferred_element_type=f32) + bg_ref[...]
    log_a = -(jnp.maximum(-x, 0.0) + jnp.log(1.0 + jnp.exp(-jnp.abs(x)))) / GLA_GATE_TAU

    r = lax.broadcasted_iota(jnp.int32, (tg, tg), 0)
    c = lax.broadcasted_iota(jnp.int32, (tg, tg), 1)
    tri = ((r // CHUNK == c // CHUNK) & (c <= r)).astype(bf16)
    hi = log_a.astype(bf16)
    rem = log_a - hi.astype(f32)
    mid = rem.astype(bf16)
    lo = (rem - mid.astype(f32)).astype(bf16)
    bcum = (jnp.dot(tri, hi, preferred_element_type=f32)
            + jnp.dot(tri, mid, preferred_element_type=f32)
            + jnp.dot(tri, lo, preferred_element_type=f32))

    n_chunks = tg // CHUNK
    q = q_ref[...].astype(f32) * (GLA_DK ** -0.5)
    k = k_ref[...].astype(f32)
    qd = (q * jnp.exp(bcum)).astype(bf16)
    kd = (k * jnp.exp(-bcum)).astype(bf16)
    b_last = [bcum[(g + 1) * CHUNK - 1:(g + 1) * CHUNK, :] for g in range(n_chunks)]
    b_last_rows = jnp.concatenate([jnp.broadcast_to(b, (CHUNK, b.shape[1])) for b in b_last], axis=0)
    ke = (k * jnp.exp(b_last_rows - bcum)).astype(bf16)
    decay = [jnp.exp(b) for b in b_last]
    same_chunk_causal = (r // CHUNK == c // CHUNK) & (c <= r)
    nt = (((1,), (1,)), ((), ()))

    def block_diag(xh):
        blocks = []
        for g in range(n_chunks):
            parts = []
            if g > 0:
                parts.append(jnp.zeros((g * CHUNK, xh.shape[1]), xh.dtype))
            parts.append(xh[g * CHUNK:(g + 1) * CHUNK, :])
            if g < n_chunks - 1:
                parts.append(jnp.zeros((tg - (g + 1) * CHUNK, xh.shape[1]), xh.dtype))
            blocks.append(jnp.concatenate(parts, axis=0))
        return jnp.concatenate(blocks, axis=1)

    heads = [(h, slice(h * GLA_DK, (h + 1) * GLA_DK), slice(h * GLA_DV, (h + 1) * GLA_DV))
             for h in range(GLA_HEADS)]
    att = [lax.dot_general(qd[:, ks], kd[:, ks], nt, preferred_element_type=f32) for h, ks, vs in heads]
    ut_all = [lax.dot_general(v_ref[:, vs], block_diag(ke[:, ks]), (((0,), (0,)), ((), ())),
                              preferred_element_type=f32) for h, ks, vs in heads]
    st_cat = []
    for h, ks, vs in heads:
        st = st_ref[h]
        st_in = []
        for g in range(n_chunks):
            st_in.append(st.astype(bf16))
            st = st * decay[g][:, ks] + ut_all[h][:, g * GLA_DK:(g + 1) * GLA_DK]
        st_ref[h] = st
        st_cat.append(jnp.concatenate(st_in, axis=1))
    for h, ks, vs in heads:
        att_h = jnp.where(same_chunk_causal, att[h], 0.0).astype(bf16)
        o = (jnp.dot(att_h, v_ref[:, vs], preferred_element_type=f32)
             + lax.dot_general(block_diag(qd[:, ks]), st_cat[h], nt, preferred_element_type=f32))
        on = _rms(o, ng_ref[:, vs])
        o_ref[:, vs] = (on * _silu(z_ref[:, vs].astype(f32))).astype(o_ref.dtype)


def _gla(proj, wg, bg, ng, B, S, tg):
    nb = S // tg
    T = B * S
    W = BRANCH_WIDTH
    QK = GLA_HEADS * GLA_DK
    rows = lambda b, j: b * nb + j
    return pl.pallas_call(
        functools.partial(_gla_kernel, tg=tg),
        grid=(B, nb),
        in_specs=[
            pl.BlockSpec((tg, QK), lambda b, j: (rows(b, j), P_QC // QK)),
            pl.BlockSpec((tg, QK), lambda b, j: (rows(b, j), P_KC // QK)),
            pl.BlockSpec((tg, W), lambda b, j: (rows(b, j), P_VC // W)),
            pl.BlockSpec((tg, 128), lambda b, j: (rows(b, j), P_GLR // 128)),
            pl.BlockSpec((tg, W), lambda b, j: (rows(b, j), P_ZC // W)),
            pl.BlockSpec(wg.shape, lambda b, j: (0, 0)),
            pl.BlockSpec((1, QK), lambda b, j: (0, 0)),
            pl.BlockSpec((1, W), lambda b, j: (0, 0)),
        ],
        out_specs=pl.BlockSpec((tg, W), lambda b, j: (rows(b, j), 0)),
        out_shape=jax.ShapeDtypeStruct((T, W), bf16),
        scratch_shapes=[pltpu.VMEM((GLA_HEADS, GLA_DV, GLA_DK), f32)],
        compiler_params=_cparams(2),
        name="gla",
    )(proj, proj, proj, proj, proj, wg, bg, ng)


def _merge_kernel(ya_ref, yb_ref, yc_ref, g0_ref, g1_ref, g2_ref, bm_ref, wb_ref, wo_ref,
                  x_ref, ng_ref, *out_refs, final):
    merged = None
    for n, (y_ref, g_ref) in enumerate(((ya_ref, g0_ref), (yb_ref, g1_ref), (yc_ref, g2_ref))):
        p = jnp.dot(y_ref[...], wb_ref[n], preferred_element_type=f32)
        gate = 1.0 / (1.0 + jnp.exp(-(g_ref[...].astype(f32) + bm_ref[n:n + 1, :])))
        merged = gate * p if merged is None else merged + gate * p
    xn = x_ref[...] + jnp.dot(merged.astype(bf16), wo_ref[...], preferred_element_type=f32)
    hn = _rms(xn, ng_ref[...])
    if final:
        out_refs[0][...] = hn
    else:
        out_refs[0][...] = xn
        out_refs[1][...] = hn.astype(bf16)


def _merge(ya, yb, yc, proj, bm, wb, wo, x, ng, tm, final):
    T, D = x.shape
    W = BRANCH_WIDTH
    row = lambda i: (i, 0)
    ytile = pl.BlockSpec((tm, W), row)
    xtile = pl.BlockSpec((tm, D), row)
    resident = dict(pipeline_mode=pl.Buffered(1))
    if final:
        out_specs = [xtile]
        out_shape = [jax.ShapeDtypeStruct((T, D), f32)]
    else:
        out_specs = [xtile, xtile]
        out_shape = [jax.ShapeDtypeStruct((T, D), f32), jax.ShapeDtypeStruct((T, D), bf16)]
    return pl.pallas_call(
        functools.partial(_merge_kernel, final=final),
        grid=(T // tm,),
        in_specs=[
            ytile, ytile, ytile,
            pl.BlockSpec((tm, D), lambda i: (i, P_GATES // D + 0)),
            pl.BlockSpec((tm, D), lambda i: (i, P_GATES // D + 1)),
            pl.BlockSpec((tm, D), lambda i: (i, P_GATES // D + 2)),
            pl.BlockSpec((N_BRANCH, D), lambda i: (0, 0)),
            pl.BlockSpec(wb.shape, lambda i: (0, 0, 0), **resident),
            pl.BlockSpec(wo.shape, lambda i: (0, 0), **resident),
            xtile,
            pl.BlockSpec((1, D), lambda i: (0, 0)),
        ],
        out_specs=out_specs,
        out_shape=out_shape,
        compiler_params=_cparams(1),
        name="merge_final" if final else "merge",
    )(ya, yb, yc, proj, proj, proj, bm, wb, wo, x, ng)


_W_IN_SLABS = (
    (P_GATES, _O_GATES, N_BRANCH * D_MODEL), (P_ZA, _O_ZA, 1024), (P_UB, _O_UB, 1024),
    (P_ZB, _O_ZB, 1024), (P_VC, _O_VC, 1024), (P_ZC, _O_ZC, 1024), (P_CQ, _O_CQ, 512),
    (P_CKV, _O_CKV, 512), (P_QC, _O_QC, 512), (P_KC, _O_KC, 512),
)


def _permute_w_in_kernel(w_ref, o_ref):
    for dst, src, n in _W_IN_SLABS:
        o_ref[0, :, dst:dst + n] = w_ref[0, :, src:src + n].astype(bf16)
    rows = w_ref.shape[1]
    half = MLA_ROPE // 2
    zero = lambda n: jnp.zeros((rows, n), bf16)
    o_ref[0, :, P_KR:P_KR + 128] = jnp.concatenate(
        [w_ref[0, :, _O_KR:_O_KR + half].astype(bf16), zero(half),
         w_ref[0, :, _O_KR + half:_O_KR + 2 * half].astype(bf16), zero(half)], axis=1)
    o_ref[0, :, P_GLR:P_GLR + 128] = jnp.concatenate(
        [w_ref[0, :, _O_GLR:_O_GLR + GLA_GATE_RANK].astype(bf16), zero(128 - GLA_GATE_RANK)], axis=1)
    o_ref[0, :, P_GLR + 128:P_TOTAL] = zero(P_TOTAL - P_GLR - 128)


def _permute_w_in(w, tr=128):
    L, D, N = w.shape
    return pl.pallas_call(
        _permute_w_in_kernel,
        grid=(L, D // tr),
        in_specs=[pl.BlockSpec((1, tr, N), lambda l, i: (l, i, 0))],
        out_specs=pl.BlockSpec((1, tr, P_TOTAL), lambda l, i: (l, i, 0)),
        out_shape=jax.ShapeDtypeStruct((L, D, P_TOTAL), bf16),
        compiler_params=_cparams(2),
        name="permute_w_in",
    )(w)


def _permute_w_uq(w):
    half = MLA_ROPE // 2
    w = w.reshape(MLA_LORA, MLA_HEADS, MLA_NOPE + MLA_ROPE)
    z = jnp.zeros((MLA_LORA, MLA_HEADS, half), w.dtype)
    out = jnp.concatenate([w[..., :MLA_NOPE], w[..., MLA_NOPE:MLA_NOPE + half], z,
                           w[..., MLA_NOPE + half:], z], axis=-1)
    return out.reshape(MLA_LORA, MLA_HEADS * HEAD_PAD).astype(bf16)


def _split_w_ukv(w):
    w = w.reshape(MLA_LORA, MLA_HEADS, MLA_NOPE + MLA_V)
    wuk = w[..., :MLA_NOPE].reshape(MLA_LORA, MLA_HEADS * MLA_NOPE).astype(bf16)
    wuvt = w[..., MLA_NOPE:].reshape(MLA_LORA, MLA_HEADS * MLA_V).T.astype(bf16)
    return wuk, wuvt


def _pick(n, pref):
    return pref if n % pref == 0 else n


def kernel(x, positions, norm_g, w_in, b_merge, mla_q_norm, mla_kv_norm, mla_w_uq, mla_w_ukv,
           pool_w, pool_scale, gla_w_gate, gla_b_gate, gla_norm, w_branch, w_out, final_norm):
    B, S, D = x.shape
    assert D == D_MODEL and S % 2048 == 0
    T = B * S
    depth = norm_g.shape[0]
    blk = 512
    tq_attn = 2048
    tm_proj = _pick(T, 1024)
    tn_proj = 1536
    tm_pool = 512
    tg = 256
    tm_merge = 256

    xf = x.reshape(T, D)
    cos_t, sin_t = _rope_tables(positions.reshape(T, 1), 512)
    h = _rmsnorm(xf, norm_g[0][None, :], 512)
    out = None
    w_in_p = _permute_w_in(w_in)
    for l in range(depth):
        wuq = _permute_w_uq(mla_w_uq[l])
        wuk, wuvt = _split_w_ukv(mla_w_ukv[l])
        wg = jnp.zeros((128, GLA_HEADS * GLA_DK), bf16).at[:GLA_GATE_RANK].set(
            gla_w_gate[l].astype(bf16))

        proj = _inproj(h, w_in_p, l, tm_proj, tn_proj)
        q, k, vt = _mla_up(proj, cos_t, sin_t, mla_q_norm[l][None, :], mla_kv_norm[l][None, :],
                           wuq, wuk, wuvt, B, S, blk)
        ya = _attention(q, k, vt, proj, B, S, tq_attn, blk)
        yb = _pool(proj, pool_w[l].astype(bf16), pool_scale[l][None, :], B, S, tm_pool)
        yc = _gla(proj, wg, gla_b_gate[l][None, :], gla_norm[l][None, :], B, S, tg)

        final = l == depth - 1
        next_g = final_norm if final else norm_g[l + 1]
        res = _merge(ya, yb, yc, proj, b_merge[l], w_branch[l].astype(bf16),
                     w_out[l].astype(bf16), xf, next_g[None, :], tm_merge, final)
        if final:
            out = res[0]
        else:
            xf, h = res
    return out.reshape(B, S, D)
```

```python
import functools
import math

import jax
import jax.numpy as jnp
import numpy as np
from jax import lax
from jax.experimental import pallas as pl
from jax.experimental.pallas import tpu as pltpu

f32 = jnp.float32
bf16 = jnp.bfloat16

D_MODEL = 2048
CHUNK = 64
EPS = 1e-6

MLA_HEADS = 8
MLA_NOPE = 128
MLA_ROPE = 64
MLA_V = 128
VT_ROWS = MLA_V + 16
FAST_MAX_JUMP = 24.0
MLA_LORA = 512
ROPE_THETA = 10000.0
HEAD_PAD = 256

POOL_WINDOWS = (2, 4, 8, 16)
POOL_GROUP_W = 256
POOL_HALO = 16

GLA_HEADS = 4
GLA_DK = 128
GLA_DV = 256
GLA_GATE_RANK = 16
GLA_GATE_TAU = 16.0

N_BRANCH = 3
BRANCH_WIDTH = 1024

_O_CQ, _O_CKV, _O_KR, _O_ZA, _O_UB, _O_ZB = 0, 512, 1024, 1088, 2112, 3136
_O_QC, _O_KC, _O_VC, _O_GLR, _O_ZC, _O_GATES = 4160, 4672, 5184, 6208, 6224, 7248

P_GATES, P_ZA, P_UB, P_ZB, P_VC, P_ZC = 0, 6144, 7168, 8192, 9216, 10240
P_CQ, P_CKV, P_QC, P_KC, P_KR, P_GLR = 11264, 11776, 12288, 12800, 13312, 13440
P_TOTAL = 13824

VMEM_LIMIT = 56 * 1024 * 1024


def _cparams(n_axes, flags=None):
    return pltpu.CompilerParams(
        dimension_semantics=("arbitrary",) * n_axes, vmem_limit_bytes=VMEM_LIMIT, flags=flags)


def _silu(z):
    return z * (1.0 / (1.0 + jnp.exp(-z)))


def _rms(x, g):
    return x * lax.rsqrt(jnp.mean(x * x, axis=-1, keepdims=True) + EPS) * g


def _rmsnorm_kernel(x_ref, g_ref, o_ref):
    o_ref[...] = _rms(x_ref[...], g_ref[...]).astype(o_ref.dtype)


def _rmsnorm(x, g, tm):
    T, D = x.shape
    return pl.pallas_call(
        _rmsnorm_kernel,
        grid=(T // tm,),
        in_specs=[pl.BlockSpec((tm, D), lambda i: (i, 0)),
                  pl.BlockSpec((1, D), lambda i: (0, 0))],
        out_specs=pl.BlockSpec((tm, D), lambda i: (i, 0)),
        out_shape=jax.ShapeDtypeStruct((T, D), bf16),
        compiler_params=_cparams(1),
        name="rmsnorm",
    )(x, g)


def _inproj_kernel(h_ref, wt_ref, o_ref):
    o_ref[...] = lax.dot_general(h_ref[...], wt_ref[...], (((1,), (1,)), ((), ())),
                                 preferred_element_type=f32).astype(o_ref.dtype)


def _inproj(h, wt, layer, tm, tn):
    T, D = h.shape
    N = wt.shape[1]
    return pl.pallas_call(
        _inproj_kernel,
        grid=(T // tm, N // tn),
        in_specs=[pl.BlockSpec((tm, D), lambda i, j: (i, 0)),
                  pl.BlockSpec((None, tn, D), lambda i, j: (layer, j, 0))],
        out_specs=pl.BlockSpec((tm, tn), lambda i, j: (i, j)),
        out_shape=jax.ShapeDtypeStruct((T, N), bf16),
        compiler_params=_cparams(2),
        name="inproj",
    )(h, wt)


def _rope_kernel(pos_ref, inv_ref, sign_ref, cos_ref, sin_ref):
    ang = pos_ref[...].astype(f32) * inv_ref[...]
    cos_ref[...] = jnp.cos(ang)
    sin_ref[...] = jnp.sin(ang) * sign_ref[...]


def _rope_tables(pos, tm):
    T = pos.shape[0]
    half = MLA_ROPE // 2
    inv = 1.0 / (ROPE_THETA ** (jnp.arange(0, MLA_ROPE, 2, dtype=f32) / MLA_ROPE))
    zero = jnp.zeros((half,), f32)
    one = jnp.ones((half,), f32)
    inv_row = jnp.concatenate([inv, zero, inv, zero])[None, :]
    sign_row = jnp.concatenate([-one, zero, one, zero])[None, :]
    row = pl.BlockSpec((1, 128), lambda i: (0, 0))
    tab = pl.BlockSpec((tm, 128), lambda i: (i, 0))
    return pl.pallas_call(
        _rope_kernel,
        grid=(T // tm,),
        in_specs=[pl.BlockSpec((tm, 1), lambda i: (i, 0)), row, row],
        out_specs=[tab, tab],
        out_shape=[jax.ShapeDtypeStruct((T, 128), f32)] * 2,
        compiler_params=_cparams(1),
        name="rope_tables",
    )(pos, inv_row, sign_row)


def _mla_up_kernel(cq_ref, ckv_ref, kr_ref, cos_ref, sin_ref, gq_ref, gkv_ref,
                   wuq_ref, wuk_ref, wuvt_ref, q_ref, k_ref, vt_ref, *, qscale):
    cos = cos_ref[...]
    sin = sin_ref[...]

    def rope(x):
        return x * cos + pltpu.roll(x, 64, 1) * sin

    cqn = _rms(cq_ref[...].astype(f32), gq_ref[...]).astype(bf16)
    qf = jnp.dot(cqn, wuq_ref[...], preferred_element_type=f32) * qscale
    for h in range(MLA_HEADS):
        c0 = h * HEAD_PAD
        q_ref[0, h, 0, :, 0:128] = qf[:, c0:c0 + 128].astype(bf16)
        q_ref[0, h, 0, :, 128:256] = rope(qf[:, c0 + 128:c0 + 256]).astype(bf16)

    ckvn = _rms(ckv_ref[...].astype(f32), gkv_ref[...]).astype(bf16)
    kn = jnp.dot(ckvn, wuk_ref[...], preferred_element_type=f32)
    krr = rope(kr_ref[...].astype(f32)).astype(bf16)
    for h in range(MLA_HEADS):
        k_ref[0, h, 0, :, 0:128] = kn[:, h * 128:(h + 1) * 128].astype(bf16)
        k_ref[0, h, 0, :, 128:256] = krr

    vt = lax.dot_general(wuvt_ref[...], ckvn, (((1,), (1,)), ((), ())),
                         preferred_element_type=f32)
    ones = jnp.ones((VT_ROWS - MLA_V, vt.shape[1]), bf16)
    for h in range(MLA_HEADS):
        vt_ref[0, h, 0, 0:MLA_V, :] = vt[h * MLA_V:(h + 1) * MLA_V, :].astype(bf16)
        vt_ref[0, h, 0, MLA_V:VT_ROWS, :] = ones


def _mla_up(proj, cos_t, sin_t, gq, gkv, wuq, wuk, wuvt, B, S, blk):
    nb = S // blk
    H = MLA_HEADS

    def rows(b, j):
        return b * nb + j

    const = lambda b, j: (0, 0)
    qk_shape = jax.ShapeDtypeStruct((B, H, nb, blk, HEAD_PAD), bf16)
    qk_spec = pl.BlockSpec((1, H, 1, blk, HEAD_PAD), lambda b, j: (b, 0, j, 0, 0))
    qscale = (MLA_NOPE + MLA_ROPE) ** -0.5 * math.log2(math.e)
    return pl.pallas_call(
        functools.partial(_mla_up_kernel, qscale=qscale),
        grid=(B, nb),
        in_specs=[
            pl.BlockSpec((blk, 512), lambda b, j: (rows(b, j), P_CQ // 512)),
            pl.BlockSpec((blk, 512), lambda b, j: (rows(b, j), P_CKV // 512)),
            pl.BlockSpec((blk, 128), lambda b, j: (rows(b, j), P_KR // 128)),
            pl.BlockSpec((blk, 128), lambda b, j: (rows(b, j), 0)),
            pl.BlockSpec((blk, 128), lambda b, j: (rows(b, j), 0)),
            pl.BlockSpec((1, MLA_LORA), const),
            pl.BlockSpec((1, MLA_LORA), const),
            pl.BlockSpec(wuq.shape, const),
            pl.BlockSpec(wuk.shape, const),
            pl.BlockSpec(wuvt.shape, const),
        ],
        out_specs=[qk_spec, qk_spec,
                   pl.BlockSpec((1, H, 1, VT_ROWS, blk), lambda b, j: (b, 0, j, 0, 0))],
        out_shape=[qk_shape, qk_shape,
                   jax.ShapeDtypeStruct((B, H, nb, VT_ROWS, blk), bf16)],
        compiler_params=_cparams(2),
        name="mla_up",
    )(proj, proj, proj, cos_t, sin_t, gq, gkv, wuq, wuk, wuvt)


def _attn_kernel(q_ref, k_ref, vt_ref, z_ref, o_ref,
                 m_ref, r_ref, risk_ref, al0_ref, al1_ref, mt0_ref, mt1_ref,
                 acc_ref, accd_ref, md_ref, s0_ref, s1_ref, p0_ref, p1_ref, *, tq, tk):
    i = pl.program_id(2)
    kpq = tq // tk
    assert kpq % 2 == 0
    n_full = i * kpq
    nt = (((1,), (1,)), ((), ()))
    m_ref[...] = jnp.full(m_ref.shape, -1e30, f32)
    acc_ref[...] = jnp.zeros(acc_ref.shape, f32)

    def scores(j, cols):
        return lax.dot_general(k_ref[0, 0, j], q_ref[0, 0, 0, cols, :], nt,
                               preferred_element_type=f32)

    def put_scores(s, bufs, cols):
        s_ref, mt_ref, _ = bufs
        s_ref[:, cols] = s
        mt_ref[:, cols] = jnp.max(s, axis=0, keepdims=True)

    def softmax_pv(j, bufs, cols):
        s_ref, mt_ref, al_ref = bufs
        m_old = m_ref[:, cols]
        m_new = jnp.maximum(m_old, mt_ref[:, cols])
        al_ref[:, cols] = jnp.exp2(m_old - m_new)
        m_ref[:, cols] = m_new
        p = jnp.exp2(s_ref[:, cols] - m_new).astype(bf16)
        acc_ref[:, cols] = (al_ref[:, cols] * acc_ref[:, cols]
                            + jnp.dot(vt_ref[0, 0, j], p, preferred_element_type=f32))

    everything = slice(0, tq)
    buf0, buf1 = (s0_ref, mt0_ref, al0_ref), (s1_ref, mt1_ref, al1_ref)

    def put_diag_scores(kt):
        cols = slice(kt * tk, tq)
        s = scores(n_full + kt, cols)
        kc = lax.broadcasted_iota(jnp.int32, s.shape, 0) // CHUNK
        qc = lax.broadcasted_iota(jnp.int32, s.shape, 1) // CHUNK
        put_scores(jnp.where(kc <= qc, s, -1e30), (buf0, buf1)[kt % 2], cols)

    put_diag_scores(0)
    for kt in range(kpq):
        if kt + 1 < kpq:
            put_diag_scores(kt + 1)
        softmax_pv(n_full + kt, (buf0, buf1)[kt % 2], slice(kt * tk, tq))

    accd_ref[...] = acc_ref[...]
    md_ref[...] = m_ref[...]
    r_ref[...] = m_ref[...]
    risk_ref[...] = jnp.zeros(risk_ref.shape, f32)
    p1_ref[...] = jnp.zeros(p1_ref.shape, bf16)
    al1_ref[...] = jnp.ones(al1_ref.shape, f32)
    fast = ((p0_ref, al0_ref), (p1_ref, al1_ref))

    def fast_pv(j, slot):
        p_ref, al_ref = fast[slot]
        acc_ref[...] = al_ref[...] * acc_ref[...] + jnp.dot(vt_ref[0, 0, j], p_ref[...],
                                                             preferred_element_type=f32)

    def fast_step(j, slot):
        p_ref, al_ref = fast[slot]
        ref = m_ref[...]
        s = scores(j, everything)
        mt = jnp.max(s, axis=0, keepdims=True)
        p_ref[...] = jnp.exp2(s - ref).astype(bf16)
        al_ref[...] = jnp.exp2(r_ref[...] - ref)
        r_ref[...] = ref
        risk_ref[...] = jnp.maximum(risk_ref[...], mt - ref)
        m_ref[...] = jnp.maximum(ref, mt)
        fast_pv(jnp.maximum(j - 1, 0), 1 - slot)

    def fast_body(t, carry):
        for u in range(kpq):
            fast_step(kpq * t + u, u % 2)
        return carry

    lax.fori_loop(0, i, fast_body, 0)
    fast_pv(jnp.maximum(n_full - 1, 0), 1)

    @pl.when(jnp.max(risk_ref[...]) > FAST_MAX_JUMP)
    def _():
        acc_ref[...] = accd_ref[...]
        m_ref[...] = md_ref[...]
        put_scores(scores(0, everything), buf0, everything)

        def body(t, carry):
            for u in range(kpq):
                j = kpq * t + u
                put_scores(scores(j + 1, everything), (buf0, buf1)[(u + 1) % 2], everything)
                softmax_pv(j, (buf0, buf1)[u % 2], everything)
            return carry

        lax.fori_loop(0, i, body, 0)

    o = acc_ref[0:MLA_V, :] / acc_ref[MLA_V:MLA_V + 1, :]
    z = z_ref[...].astype(f32)
    o_ref[...] = (o.T * _silu(z)).astype(o_ref.dtype)


def _attention(q, k, vt, proj, B, S, tq, tk):
    H = MLA_HEADS
    nq, nk = S // tq, S // tk
    T = B * S
    q = q.reshape(B, H, nq, tq, HEAD_PAD)
    row = pltpu.VMEM((1, tq), f32)
    acc = pltpu.VMEM((VT_ROWS, tq), f32)
    return pl.pallas_call(
        functools.partial(_attn_kernel, tq=tq, tk=tk),
        grid=(B, H, nq),
        in_specs=[
            pl.BlockSpec((1, 1, 1, tq, HEAD_PAD), lambda b, h, i: (b, h, i, 0, 0)),
            pl.BlockSpec((1, 1, nk, tk, HEAD_PAD), lambda b, h, i: (b, h, 0, 0, 0)),
            pl.BlockSpec((1, 1, nk, VT_ROWS, tk), lambda b, h, i: (b, h, 0, 0, 0)),
            pl.BlockSpec((tq, MLA_V), lambda b, h, i: (b * nq + i, P_ZA // MLA_V + h)),
        ],
        out_specs=pl.BlockSpec((tq, MLA_V), lambda b, h, i: (b * nq + i, h)),
        out_shape=jax.ShapeDtypeStruct((T, BRANCH_WIDTH), bf16),
        scratch_shapes=[row] * 7 + [acc, acc, row,
                                    pltpu.VMEM((tk, tq), f32), pltpu.VMEM((tk, tq), f32),
                                    pltpu.VMEM((tk, tq), bf16), pltpu.VMEM((tk, tq), bf16)],
        compiler_params=_cparams(3),
        name="mla_attention",
    )(q, k, vt, proj)


def _pool_kernel(u_ref, z_ref, bm_ref, bh_ref, w_ref, sc_ref, o_ref, halo_ref, *, tm):
    j = pl.program_id(1)
    H = POOL_HALO

    @pl.when(j == 0)
    def _():
        halo_ref[...] = jnp.zeros(halo_ref.shape, bf16)

    u = u_ref[...]
    halo = halo_ref[...]
    t = j * tm + lax.broadcasted_iota(jnp.int32, (tm, 1), 0)
    groups = [(g, w, slice(g * POOL_GROUP_W, (g + 1) * POOL_GROUP_W)) for g, w in enumerate(POOL_WINDOWS)]
    sums = [jnp.dot(bm_ref[g], u[:, cs], preferred_element_type=f32)
            + jnp.dot(bh_ref[g], halo[:, cs], preferred_element_type=f32) for g, w, cs in groups]
    means = [(sums[g] / jnp.minimum(t + 1, w).astype(f32) - u[:, cs].astype(f32)).astype(bf16)
             for g, w, cs in groups]
    ys = [jnp.dot(means[g], w_ref[g], preferred_element_type=f32) for g, w, cs in groups]
    for g, w, cs in groups:
        o_ref[:, cs] = (ys[g] * sc_ref[:, cs] * _silu(z_ref[:, cs].astype(f32))).astype(o_ref.dtype)
    halo_ref[...] = u[tm - H:tm, :]


def _pool_bands(tm):
    t = np.arange(tm)[:, None]
    main = np.stack([((t - np.arange(tm)[None, :] >= 0) & (t - np.arange(tm)[None, :] < w))
                     for w in POOL_WINDOWS])
    halo = np.stack([(t + POOL_HALO - np.arange(POOL_HALO)[None, :] < w) for w in POOL_WINDOWS])
    return jnp.asarray(main, bf16), jnp.asarray(halo, bf16)


def _pool(proj, pool_w, pool_scale, B, S, tm):
    nb = S // tm
    T = B * S
    W = BRANCH_WIDTH
    band_main, band_halo = _pool_bands(tm)
    const3 = lambda b, j: (0, 0, 0)
    return pl.pallas_call(
        functools.partial(_pool_kernel, tm=tm),
        grid=(B, nb),
        in_specs=[
            pl.BlockSpec((tm, W), lambda b, j: (b * nb + j, P_UB // W)),
            pl.BlockSpec((tm, W), lambda b, j: (b * nb + j, P_ZB // W)),
            pl.BlockSpec(band_main.shape, const3),
            pl.BlockSpec(band_halo.shape, const3),
            pl.BlockSpec(pool_w.shape, const3),
            pl.BlockSpec((1, W), lambda b, j: (0, 0)),
        ],
        out_specs=pl.BlockSpec((tm, W), lambda b, j: (b * nb + j, 0)),
        out_shape=jax.ShapeDtypeStruct((T, W), bf16),
        scratch_shapes=[pltpu.VMEM((POOL_HALO, W), bf16)],
        compiler_params=_cparams(2),
        name="pool",
    )(proj, proj, band_main, band_halo, pool_w, pool_scale)


def _gla_kernel(q_ref, k_ref, v_ref, glr_ref, z_ref, wg_ref, bg_ref, ng_ref, o_ref,
                st_ref, *, tg):
    j = pl.program_id(1)

    @pl.when(j == 0)
    def _():
        st_ref[...] = jnp.zeros(st_ref.shape, f32)

    x = jnp.dot(glr_ref[...], wg_ref[...], preferred_element_type=f32) + bg_ref[...]
    log_a = -(jnp.maximum(-x, 0.0) + jnp.log(1.0 + jnp.exp(-jnp.abs(x)))) / GLA_GATE_TAU

    r = lax.broadcasted_iota(jnp.int32, (tg, tg), 0)
    c = lax.broadcasted_iota(jnp.int32, (tg, tg), 1)
    tri = ((r // CHUNK == c // CHUNK) & (c <= r)).astype(bf16)
    hi = log_a.astype(bf16)
    rem = log_a - hi.astype(f32)
    mid = rem.astype(bf16)
    lo = (rem - mid.astype(f32)).astype(bf16)
    bcum = (jnp.dot(tri, hi, preferred_element_type=f32)
            + jnp.dot(tri, mid, preferred_element_type=f32)
            + jnp.dot(tri, lo, preferred_element_type=f32))

    n_chunks = tg // CHUNK
    q = q_ref[...].astype(f32) * (GLA_DK ** -0.5)
    k = k_ref[...].astype(f32)
    qd = (q * jnp.exp(bcum)).astype(bf16)
    kd = (k * jnp.exp(-bcum)).astype(bf16)
    b_last = [bcum[(g + 1) * CHUNK - 1:(g + 1) * CHUNK, :] for g in range(n_chunks)]
    b_last_rows = jnp.concatenate([jnp.broadcast_to(b, (CHUNK, b.shape[1])) for b in b_last], axis=0)
    ke = (k * jnp.exp(b_last_rows - bcum)).astype(bf16)
    decay = [jnp.exp(b) for b in b_last]
    same_chunk_causal = (r // CHUNK == c // CHUNK) & (c <= r)
    nt = (((1,), (1,)), ((), ()))

    def block_diag(xh):
        blocks = []
        for g in range(n_chunks):
            parts = []
            if g > 0:
                parts.append(jnp.zeros((g * CHUNK, xh.shape[1]), xh.dtype))
            parts.append(xh[g * CHUNK:(g + 1) * CHUNK, :])
            if g < n_chunks - 1:
                parts.append(jnp.zeros((tg - (g + 1) * CHUNK, xh.shape[1]), xh.dtype))
            blocks.append(jnp.concatenate(parts, axis=0))
        return jnp.concatenate(blocks, axis=1)

    heads = [(h, slice(h * GLA_DK, (h + 1) * GLA_DK), slice(h * GLA_DV, (h + 1) * GLA_DV))
             for h in range(GLA_HEADS)]
    att = [lax.dot_general(qd[:, ks], kd[:, ks], nt, preferred_element_type=f32) for h, ks, vs in heads]
    ut_all = [lax.dot_general(v_ref[:, vs], block_diag(ke[:, ks]), (((0,), (0,)), ((), ())),
                              preferred_element_type=f32) for h, ks, vs in heads]
    st_cat = []
    for h, ks, vs in heads:
        st = st_ref[h]
        st_in = []
        for g in range(n_chunks):
            st_in.append(st.astype(bf16))
            st = st * decay[g][:, ks] + ut_all[h][:, g * GLA_DK:(g + 1) * GLA_DK]
        st_ref[h] = st
        st_cat.append(jnp.concatenate(st_in, axis=1))
    for h, ks, vs in heads:
        att_h = jnp.where(same_chunk_causal, att[h], 0.0).astype(bf16)
        o = (jnp.dot(att_h, v_ref[:, vs], preferred_element_type=f32)
             + lax.dot_general(block_diag(qd[:, ks]), st_cat[h], nt, preferred_element_type=f32))
        on = _rms(o, ng_ref[:, vs])
        o_ref[:, vs] = (on * _silu(z_ref[:, vs].astype(f32))).astype(o_ref.dtype)


def _gla(proj, wg, bg, ng, B, S, tg):
    nb = S // tg
    T = B * S
    W = BRANCH_WIDTH
    QK = GLA_HEADS * GLA_DK
    rows = lambda b, j: b * nb + j
    return pl.pallas_call(
        functools.partial(_gla_kernel, tg=tg),
        grid=(B, nb),
        in_specs=[
            pl.BlockSpec((tg, QK), lambda b, j: (rows(b, j), P_QC // QK)),
            pl.BlockSpec((tg, QK), lambda b, j: (rows(b, j), P_KC // QK)),
            pl.BlockSpec((tg, W), lambda b, j: (rows(b, j), P_VC // W)),
            pl.BlockSpec((tg, 128), lambda b, j: (rows(b, j), P_GLR // 128)),
            pl.BlockSpec((tg, W), lambda b, j: (rows(b, j), P_ZC // W)),
            pl.BlockSpec(wg.shape, lambda b, j: (0, 0)),
            pl.BlockSpec((1, QK), lambda b, j: (0, 0)),
            pl.BlockSpec((1, W), lambda b, j: (0, 0)),
        ],
        out_specs=pl.BlockSpec((tg, W), lambda b, j: (rows(b, j), 0)),
        out_shape=jax.ShapeDtypeStruct((T, W), bf16),
        scratch_shapes=[pltpu.VMEM((GLA_HEADS, GLA_DV, GLA_DK), f32)],
        compiler_params=_cparams(2),
        name="gla",
    )(proj, proj, proj, proj, proj, wg, bg, ng)


def _merge_kernel(ya_ref, yb_ref, yc_ref, g0_ref, g1_ref, g2_ref, bm_ref, wb_ref, wo_ref,
                  x_ref, ng_ref, *out_refs, final):
    merged = None
    for n, (y_ref, g_ref) in enumerate(((ya_ref, g0_ref), (yb_ref, g1_ref), (yc_ref, g2_ref))):
        p = jnp.dot(y_ref[...], wb_ref[n], preferred_element_type=f32)
        gate = 1.0 / (1.0 + jnp.exp(-(g_ref[...].astype(f32) + bm_ref[n:n + 1, :])))
        merged = gate * p if merged is None else merged + gate * p
    xn = x_ref[...] + jnp.dot(merged.astype(bf16), wo_ref[...], preferred_element_type=f32)
    hn = _rms(xn, ng_ref[...])
    if final:
        out_refs[0][...] = hn
    else:
        out_refs[0][...] = xn
        out_refs[1][...] = hn.astype(bf16)


def _merge(ya, yb, yc, proj, bm, wb, wo, x, ng, tm, final):
    T, D = x.shape
    W = BRANCH_WIDTH
    row = lambda i: (i, 0)
    ytile = pl.BlockSpec((tm, W), row)
    xtile = pl.BlockSpec((tm, D), row)
    resident = dict(pipeline_mode=pl.Buffered(1))
    if final:
        out_specs = [xtile]
        out_shape = [jax.ShapeDtypeStruct((T, D), f32)]
    else:
        out_specs = [xtile, xtile]
        out_shape = [jax.ShapeDtypeStruct((T, D), f32), jax.ShapeDtypeStruct((T, D), bf16)]
    return pl.pallas_call(
        functools.partial(_merge_kernel, final=final),
        grid=(T // tm,),
        in_specs=[
            ytile, ytile, ytile,
            pl.BlockSpec((tm, D), lambda i: (i, P_GATES // D + 0)),
            pl.BlockSpec((tm, D), lambda i: (i, P_GATES // D + 1)),
            pl.BlockSpec((tm, D), lambda i: (i, P_GATES // D + 2)),
            pl.BlockSpec((N_BRANCH, D), lambda i: (0, 0)),
            pl.BlockSpec(wb.shape, lambda i: (0, 0, 0), **resident),
            pl.BlockSpec(wo.shape, lambda i: (0, 0), **resident),
            xtile,
            pl.BlockSpec((1, D), lambda i: (0, 0)),
        ],
        out_specs=out_specs,
        out_shape=out_shape,
        compiler_params=_cparams(1),
        name="merge_final" if final else "merge",
    )(ya, yb, yc, proj, proj, proj, bm, wb, wo, x, ng)


_W_IN_SLABS = (
    (P_GATES, _O_GATES, N_BRANCH * D_MODEL), (P_ZA, _O_ZA, 1024), (P_UB, _O_UB, 1024),
    (P_ZB, _O_ZB, 1024), (P_VC, _O_VC, 1024), (P_ZC, _O_ZC, 1024), (P_CQ, _O_CQ, 512),
    (P_CKV, _O_CKV, 512), (P_QC, _O_QC, 512), (P_KC, _O_KC, 512),
)


def _permute_w_in_kernel(w_ref, o_ref):
    for dst, src, n in _W_IN_SLABS:
        o_ref[0, dst:dst + n, :] = w_ref[0, src:src + n, :].astype(bf16)
    cols = w_ref.shape[2]
    half = MLA_ROPE // 2
    zero = lambda n: jnp.zeros((n, cols), bf16)
    for part, src in enumerate((_O_KR, _O_KR + half)):
        o_ref[0, P_KR + 2 * part * half:P_KR + (2 * part + 1) * half, :] = w_ref[0, src:src + half, :].astype(bf16)
        o_ref[0, P_KR + (2 * part + 1) * half:P_KR + (2 * part + 2) * half, :] = zero(half)
    o_ref[0, P_GLR:P_GLR + GLA_GATE_RANK, :] = w_ref[0, _O_GLR:_O_GLR + GLA_GATE_RANK, :].astype(bf16)
    o_ref[0, P_GLR + GLA_GATE_RANK:P_TOTAL, :] = zero(P_TOTAL - P_GLR - GLA_GATE_RANK)


def _permute_w_in(w, tc=128):
    L, D, N = w.shape
    return pl.pallas_call(
        _permute_w_in_kernel,
        grid=(L, D // tc),
        in_specs=[pl.BlockSpec((1, N, tc), lambda l, i: (l, 0, i))],
        out_specs=pl.BlockSpec((1, P_TOTAL, tc), lambda l, i: (l, 0, i)),
        out_shape=jax.ShapeDtypeStruct((L, P_TOTAL, D), bf16),
        compiler_params=_cparams(2),
        name="permute_w_in",
    )(jnp.swapaxes(w, 1, 2))


def _permute_w_uq(w):
    half = MLA_ROPE // 2
    w = w.reshape(MLA_LORA, MLA_HEADS, MLA_NOPE + MLA_ROPE)
    z = jnp.zeros((MLA_LORA, MLA_HEADS, half), w.dtype)
    out = jnp.concatenate([w[..., :MLA_NOPE], w[..., MLA_NOPE:MLA_NOPE + half], z,
                           w[..., MLA_NOPE + half:], z], axis=-1)
    return out.reshape(MLA_LORA, MLA_HEADS * HEAD_PAD).astype(bf16)


def _split_w_ukv(w):
    w = w.reshape(MLA_LORA, MLA_HEADS, MLA_NOPE + MLA_V)
    wuk = w[..., :MLA_NOPE].reshape(MLA_LORA, MLA_HEADS * MLA_NOPE).astype(bf16)
    wuvt = w[..., MLA_NOPE:].reshape(MLA_LORA, MLA_HEADS * MLA_V).T.astype(bf16)
    return wuk, wuvt


def _pick(n, pref):
    return pref if n % pref == 0 else n


def kernel(x, positions, norm_g, w_in, b_merge, mla_q_norm, mla_kv_norm, mla_w_uq, mla_w_ukv,
           pool_w, pool_scale, gla_w_gate, gla_b_gate, gla_norm, w_branch, w_out, final_norm):
    B, S, D = x.shape
    assert D == D_MODEL and S % 2048 == 0
    T = B * S
    depth = norm_g.shape[0]
    blk = 512
    tq_attn = 2048
    tm_proj = _pick(T, 1024)
    tn_proj = 1536
    tm_pool = 512
    tg = 256
    tm_merge = 256

    xf = x.reshape(T, D)
    cos_t, sin_t = _rope_tables(positions.reshape(T, 1), 512)
    h = _rmsnorm(xf, norm_g[0][None, :], 512)
    out = None
    w_in_p = _permute_w_in(w_in)
    for l in range(depth):
        wuq = _permute_w_uq(mla_w_uq[l])
        wuk, wuvt = _split_w_ukv(mla_w_ukv[l])
        wg = jnp.zeros((128, GLA_HEADS * GLA_DK), bf16).at[:GLA_GATE_RANK].set(
            gla_w_gate[l].astype(bf16))

        proj = _inproj(h, w_in_p, l, tm_proj, tn_proj)
        q, k, vt = _mla_up(proj, cos_t, sin_t, mla_q_norm[l][None, :], mla_kv_norm[l][None, :],
                           wuq, wuk, wuvt, B, S, blk)
        ya = _attention(q, k, vt, proj, B, S, tq_attn, blk)
        yb = _pool(proj, pool_w[l].astype(bf16), pool_scale[l][None, :], B, S, tm_pool)
        yc = _gla(proj, wg, gla_b_gate[l][None, :], gla_norm[l][None, :], B, S, tg)

        final = l == depth - 1
        next_g = final_norm if final else norm_g[l + 1]
        res = _merge(ya, yb, yc, proj, b_merge[l], w_branch[l].astype(bf16),
                     w_out[l].astype(bf16), xf, next_g[None, :], tm_merge, final)
        if final:
            out = res[0]
        else:
            xf, h = res
    return out.reshape(B, S, D)
```

```python
import functools
import math

import jax
import jax.numpy as jnp
import numpy as np
from jax import lax
from jax.experimental import pallas as pl
from jax.experimental.pallas import tpu as pltpu

f32 = jnp.float32
bf16 = jnp.bfloat16

D_MODEL = 2048
CHUNK = 64
EPS = 1e-6

MLA_HEADS = 8
MLA_NOPE = 128
MLA_ROPE = 64
MLA_V = 128
VT_ROWS = MLA_V + 16
FAST_MAX_JUMP = 24.0
MLA_LORA = 512
ROPE_THETA = 10000.0
HEAD_PAD = 256

POOL_WINDOWS = (2, 4, 8, 16)
POOL_GROUP_W = 256
POOL_HALO = 16

GLA_HEADS = 4
GLA_DK = 128
GLA_DV = 256
GLA_GATE_RANK = 16
GLA_GATE_TAU = 16.0

N_BRANCH = 3
BRANCH_WIDTH = 1024

_O_CQ, _O_CKV, _O_KR, _O_ZA, _O_UB, _O_ZB = 0, 512, 1024, 1088, 2112, 3136
_O_QC, _O_KC, _O_VC, _O_GLR, _O_ZC, _O_GATES = 4160, 4672, 5184, 6208, 6224, 7248

P_GATES, P_ZA, P_UB, P_ZB, P_VC, P_ZC = 0, 6144, 7168, 8192, 9216, 10240
P_CQ, P_CKV, P_QC, P_KC, P_KR, P_GLR = 11264, 11776, 12288, 12800, 13312, 13440
P_TOTAL = 13824

VMEM_LIMIT = 56 * 1024 * 1024


def _cparams(n_axes, flags=None):
    return pltpu.CompilerParams(
        dimension_semantics=("arbitrary",) * n_axes, vmem_limit_bytes=VMEM_LIMIT, flags=flags)


def _silu(z):
    return z * (1.0 / (1.0 + jnp.exp(-z)))


def _rms(x, g):
    return x * lax.rsqrt(jnp.mean(x * x, axis=-1, keepdims=True) + EPS) * g


def _rmsnorm_kernel(x_ref, g_ref, o_ref):
    o_ref[...] = _rms(x_ref[...], g_ref[...]).astype(o_ref.dtype)


def _rmsnorm(x, g, tm):
    T, D = x.shape
    return pl.pallas_call(
        _rmsnorm_kernel,
        grid=(T // tm,),
        in_specs=[pl.BlockSpec((tm, D), lambda i: (i, 0)),
                  pl.BlockSpec((1, D), lambda i: (0, 0))],
        out_specs=pl.BlockSpec((tm, D), lambda i: (i, 0)),
        out_shape=jax.ShapeDtypeStruct((T, D), bf16),
        compiler_params=_cparams(1),
        name="rmsnorm",
    )(x, g)


def _inproj_kernel(h_ref, wt_ref, o_ref):
    o_ref[...] = lax.dot_general(h_ref[...], wt_ref[...], (((1,), (1,)), ((), ())),
                                 preferred_element_type=f32).astype(o_ref.dtype)


def _inproj(h, wt, layer, tm, tn):
    T, D = h.shape
    N = wt.shape[1]
    return pl.pallas_call(
        _inproj_kernel,
        grid=(T // tm, N // tn),
        in_specs=[pl.BlockSpec((tm, D), lambda i, j: (i, 0)),
                  pl.BlockSpec((None, tn, D), lambda i, j: (layer, j, 0))],
        out_specs=pl.BlockSpec((tm, tn), lambda i, j: (i, j)),
        out_shape=jax.ShapeDtypeStruct((T, N), bf16),
        compiler_params=_cparams(2),
        name="inproj",
    )(h, wt)


def _rope_kernel(pos_ref, inv_ref, sign_ref, cos_ref, sin_ref):
    ang = pos_ref[...].astype(f32) * inv_ref[...]
    cos_ref[...] = jnp.cos(ang)
    sin_ref[...] = jnp.sin(ang) * sign_ref[...]


def _rope_tables(pos, tm):
    T = pos.shape[0]
    half = MLA_ROPE // 2
    inv = 1.0 / (ROPE_THETA ** (jnp.arange(0, MLA_ROPE, 2, dtype=f32) / MLA_ROPE))
    zero = jnp.zeros((half,), f32)
    one = jnp.ones((half,), f32)
    inv_row = jnp.concatenate([inv, zero, inv, zero])[None, :]
    sign_row = jnp.concatenate([-one, zero, one, zero])[None, :]
    row = pl.BlockSpec((1, 128), lambda i: (0, 0))
    tab = pl.BlockSpec((tm, 128), lambda i: (i, 0))
    return pl.pallas_call(
        _rope_kernel,
        grid=(T // tm,),
        in_specs=[pl.BlockSpec((tm, 1), lambda i: (i, 0)), row, row],
        out_specs=[tab, tab],
        out_shape=[jax.ShapeDtypeStruct((T, 128), f32)] * 2,
        compiler_params=_cparams(1),
        name="rope_tables",
    )(pos, inv_row, sign_row)


def _mla_up_kernel(cq_ref, ckv_ref, kr_ref, cos_ref, sin_ref, gq_ref, gkv_ref,
                   wuq_ref, wuk_ref, wuvt_ref, q_ref, k_ref, vt_ref, *, qscale):
    cos = cos_ref[...]
    sin = sin_ref[...]

    def rope(x):
        return x * cos + pltpu.roll(x, 64, 1) * sin

    cqn = _rms(cq_ref[...].astype(f32), gq_ref[...]).astype(bf16)
    qf = jnp.dot(cqn, wuq_ref[...], preferred_element_type=f32) * qscale
    for h in range(MLA_HEADS):
        c0 = h * HEAD_PAD
        q_ref[0, h, 0, :, 0:128] = qf[:, c0:c0 + 128].astype(bf16)
        q_ref[0, h, 0, :, 128:256] = rope(qf[:, c0 + 128:c0 + 256]).astype(bf16)

    ckvn = _rms(ckv_ref[...].astype(f32), gkv_ref[...]).astype(bf16)
    kn = jnp.dot(ckvn, wuk_ref[...], preferred_element_type=f32)
    krr = rope(kr_ref[...].astype(f32)).astype(bf16)
    for h in range(MLA_HEADS):
        k_ref[0, h, 0, :, 0:128] = kn[:, h * 128:(h + 1) * 128].astype(bf16)
        k_ref[0, h, 0, :, 128:256] = krr

    vt = lax.dot_general(wuvt_ref[...], ckvn, (((1,), (1,)), ((), ())),
                         preferred_element_type=f32)
    ones = jnp.ones((VT_ROWS - MLA_V, vt.shape[1]), bf16)
    for h in range(MLA_HEADS):
        vt_ref[0, h, 0, 0:MLA_V, :] = vt[h * MLA_V:(h + 1) * MLA_V, :].astype(bf16)
        vt_ref[0, h, 0, MLA_V:VT_ROWS, :] = ones


def _mla_up(proj, cos_t, sin_t, gq, gkv, wuq, wuk, wuvt, B, S, blk):
    nb = S // blk
    H = MLA_HEADS

    def rows(b, j):
        return b * nb + j

    const = lambda b, j: (0, 0)
    qk_shape = jax.ShapeDtypeStruct((B, H, nb, blk, HEAD_PAD), bf16)
    qk_spec = pl.BlockSpec((1, H, 1, blk, HEAD_PAD), lambda b, j: (b, 0, j, 0, 0))
    qscale = (MLA_NOPE + MLA_ROPE) ** -0.5 * math.log2(math.e)
    return pl.pallas_call(
        functools.partial(_mla_up_kernel, qscale=qscale),
        grid=(B, nb),
        in_specs=[
            pl.BlockSpec((blk, 512), lambda b, j: (rows(b, j), P_CQ // 512)),
            pl.BlockSpec((blk, 512), lambda b, j: (rows(b, j), P_CKV // 512)),
            pl.BlockSpec((blk, 128), lambda b, j: (rows(b, j), P_KR // 128)),
            pl.BlockSpec((blk, 128), lambda b, j: (rows(b, j), 0)),
            pl.BlockSpec((blk, 128), lambda b, j: (rows(b, j), 0)),
            pl.BlockSpec((1, MLA_LORA), const),
            pl.BlockSpec((1, MLA_LORA), const),
            pl.BlockSpec(wuq.shape, const),
            pl.BlockSpec(wuk.shape, const),
            pl.BlockSpec(wuvt.shape, const),
        ],
        out_specs=[qk_spec, qk_spec,
                   pl.BlockSpec((1, H, 1, VT_ROWS, blk), lambda b, j: (b, 0, j, 0, 0))],
        out_shape=[qk_shape, qk_shape,
                   jax.ShapeDtypeStruct((B, H, nb, VT_ROWS, blk), bf16)],
        compiler_params=_cparams(2),
        name="mla_up",
    )(proj, proj, proj, cos_t, sin_t, gq, gkv, wuq, wuk, wuvt)


def _attn_kernel(q_ref, k_ref, vt_ref, z_ref, o_ref,
                 m_ref, r_ref, risk_ref, al0_ref, al1_ref, mt0_ref, mt1_ref,
                 acc_ref, accd_ref, md_ref, s0_ref, s1_ref, p0_ref, p1_ref, *, tq, tk):
    i = pl.program_id(2)
    kpq = tq // tk
    assert kpq % 2 == 0
    n_full = i * kpq
    nt = (((1,), (1,)), ((), ()))
    m_ref[...] = jnp.full(m_ref.shape, -1e30, f32)
    acc_ref[...] = jnp.zeros(acc_ref.shape, f32)

    def scores(j, cols):
        return lax.dot_general(k_ref[0, 0, j], q_ref[0, 0, 0, cols, :], nt,
                               preferred_element_type=f32)

    def put_scores(s, bufs, cols):
        s_ref, mt_ref, _ = bufs
        s_ref[:, cols] = s
        mt_ref[:, cols] = jnp.max(s, axis=0, keepdims=True)

    def softmax_pv(j, bufs, cols):
        s_ref, mt_ref, al_ref = bufs
        m_old = m_ref[:, cols]
        m_new = jnp.maximum(m_old, mt_ref[:, cols])
        al_ref[:, cols] = jnp.exp2(m_old - m_new)
        m_ref[:, cols] = m_new
        p = jnp.exp2(s_ref[:, cols] - m_new).astype(bf16)
        acc_ref[:, cols] = (al_ref[:, cols] * acc_ref[:, cols]
                            + jnp.dot(vt_ref[0, 0, j], p, preferred_element_type=f32))

    everything = slice(0, tq)
    buf0, buf1 = (s0_ref, mt0_ref, al0_ref), (s1_ref, mt1_ref, al1_ref)

    def put_diag_scores(kt):
        cols = slice(kt * tk, tq)
        s = scores(n_full + kt, cols)
        kc = lax.broadcasted_iota(jnp.int32, s.shape, 0) // CHUNK
        qc = lax.broadcasted_iota(jnp.int32, s.shape, 1) // CHUNK
        put_scores(jnp.where(kc <= qc, s, -1e30), (buf0, buf1)[kt % 2], cols)

    put_diag_scores(0)
    for kt in range(kpq):
        if kt + 1 < kpq:
            put_diag_scores(kt + 1)
        softmax_pv(n_full + kt, (buf0, buf1)[kt % 2], slice(kt * tk, tq))

    accd_ref[...] = acc_ref[...]
    md_ref[...] = m_ref[...]
    r_ref[...] = m_ref[...]
    risk_ref[...] = jnp.zeros(risk_ref.shape, f32)
    p1_ref[...] = jnp.zeros(p1_ref.shape, bf16)
    al1_ref[...] = jnp.ones(al1_ref.shape, f32)
    fast = ((p0_ref, al0_ref), (p1_ref, al1_ref))

    def fast_pv(j, slot):
        p_ref, al_ref = fast[slot]
        acc_ref[...] = al_ref[...] * acc_ref[...] + jnp.dot(vt_ref[0, 0, j], p_ref[...],
                                                             preferred_element_type=f32)

    def fast_step(j, slot):
        p_ref, al_ref = fast[slot]
        ref = m_ref[...]
        s = scores(j, everything)
        mt = jnp.max(s, axis=0, keepdims=True)
        p_ref[...] = jnp.exp2(s - ref).astype(bf16)
        al_ref[...] = jnp.exp2(r_ref[...] - ref)
        r_ref[...] = ref
        risk_ref[...] = jnp.maximum(risk_ref[...], mt - ref)
        m_ref[...] = jnp.maximum(ref, mt)
        fast_pv(jnp.maximum(j - 1, 0), 1 - slot)

    def fast_body(t, carry):
        for u in range(kpq):
            fast_step(kpq * t + u, u % 2)
        return carry

    lax.fori_loop(0, i, fast_body, 0)
    fast_pv(jnp.maximum(n_full - 1, 0), 1)

    @pl.when(jnp.max(risk_ref[...]) > FAST_MAX_JUMP)
    def _():
        acc_ref[...] = accd_ref[...]
        m_ref[...] = md_ref[...]
        put_scores(scores(0, everything), buf0, everything)

        def body(t, carry):
            for u in range(kpq):
                j = kpq * t + u
                put_scores(scores(j + 1, everything), (buf0, buf1)[(u + 1) % 2], everything)
                softmax_pv(j, (buf0, buf1)[u % 2], everything)
            return carry

        lax.fori_loop(0, i, body, 0)

    o = acc_ref[0:MLA_V, :] / acc_ref[MLA_V:MLA_V + 1, :]
    z = z_ref[...].astype(f32)
    o_ref[...] = (o.T * _silu(z)).astype(o_ref.dtype)


def _attention(q, k, vt, proj, B, S, tq, tk):
    H = MLA_HEADS
    nq, nk = S // tq, S // tk
    T = B * S
    q = q.reshape(B, H, nq, tq, HEAD_PAD)
    row = pltpu.VMEM((1, tq), f32)
    acc = pltpu.VMEM((VT_ROWS, tq), f32)
    return pl.pallas_call(
        functools.partial(_attn_kernel, tq=tq, tk=tk),
        grid=(B, H, nq),
        in_specs=[
            pl.BlockSpec((1, 1, 1, tq, HEAD_PAD), lambda b, h, i: (b, h, i, 0, 0)),
            pl.BlockSpec((1, 1, nk, tk, HEAD_PAD), lambda b, h, i: (b, h, 0, 0, 0)),
            pl.BlockSpec((1, 1, nk, VT_ROWS, tk), lambda b, h, i: (b, h, 0, 0, 0)),
            pl.BlockSpec((tq, MLA_V), lambda b, h, i: (b * nq + i, P_ZA // MLA_V + h)),
        ],
        out_specs=pl.BlockSpec((tq, MLA_V), lambda b, h, i: (b * nq + i, h)),
        out_shape=jax.ShapeDtypeStruct((T, BRANCH_WIDTH), bf16),
        scratch_shapes=[row] * 7 + [acc, acc, row,
                                    pltpu.VMEM((tk, tq), f32), pltpu.VMEM((tk, tq), f32),
                                    pltpu.VMEM((tk, tq), bf16), pltpu.VMEM((tk, tq), bf16)],
        compiler_params=_cparams(3),
        name="mla_attention",
    )(q, k, vt, proj)


def _pool_body(u_ref, z_ref, bm_ref, bh_ref, w_ref, sc_ref, o_ref, halo_ref, *, tm):
    j = pl.program_id(1)
    H = POOL_HALO
    u = u_ref[...]
    halo = halo_ref[...]
    t = j * tm + lax.broadcasted_iota(jnp.int32, (tm, 1), 0)
    groups = [(g, w, slice(g * POOL_GROUP_W, (g + 1) * POOL_GROUP_W)) for g, w in enumerate(POOL_WINDOWS)]
    sums = [jnp.dot(bm_ref[g], u[:, cs], preferred_element_type=f32)
            + jnp.dot(bh_ref[g], halo[:, cs], preferred_element_type=f32) for g, w, cs in groups]
    means = [(sums[g] / jnp.minimum(t + 1, w).astype(f32) - u[:, cs].astype(f32)).astype(bf16)
             for g, w, cs in groups]
    ys = [jnp.dot(means[g], w_ref[g], preferred_element_type=f32) for g, w, cs in groups]
    for g, w, cs in groups:
        o_ref[:, cs] = (ys[g] * sc_ref[:, cs] * _silu(z_ref[:, cs].astype(f32))).astype(o_ref.dtype)
    halo_ref[...] = u[tm - H:tm, :]


def _pool_bands(tm):
    t = np.arange(tm)[:, None]
    main = np.stack([((t - np.arange(tm)[None, :] >= 0) & (t - np.arange(tm)[None, :] < w))
                     for w in POOL_WINDOWS])
    halo = np.stack([(t + POOL_HALO - np.arange(POOL_HALO)[None, :] < w) for w in POOL_WINDOWS])
    return jnp.asarray(main, bf16), jnp.asarray(halo, bf16)


def _gla_body(q_ref, k_ref, v_ref, glr_ref, z_ref, wg_ref, bg_ref, ng_ref, o_ref,
              st_ref, *, tg):
    x = jnp.dot(glr_ref[...], wg_ref[...], preferred_element_type=f32) + bg_ref[...]
    log_a = -(jnp.maximum(-x, 0.0) + jnp.log(1.0 + jnp.exp(-jnp.abs(x)))) / GLA_GATE_TAU

    r = lax.broadcasted_iota(jnp.int32, (tg, tg), 0)
    c = lax.broadcasted_iota(jnp.int32, (tg, tg), 1)
    tri = ((r // CHUNK == c // CHUNK) & (c <= r)).astype(bf16)
    hi = log_a.astype(bf16)
    rem = log_a - hi.astype(f32)
    mid = rem.astype(bf16)
    lo = (rem - mid.astype(f32)).astype(bf16)
    bcum = (jnp.dot(tri, hi, preferred_element_type=f32)
            + jnp.dot(tri, mid, preferred_element_type=f32)
            + jnp.dot(tri, lo, preferred_element_type=f32))

    n_chunks = tg // CHUNK
    q = q_ref[...].astype(f32) * (GLA_DK ** -0.5)
    k = k_ref[...].astype(f32)
    qd = (q * jnp.exp(bcum)).astype(bf16)
    kd = (k * jnp.exp(-bcum)).astype(bf16)
    b_last = [bcum[(g + 1) * CHUNK - 1:(g + 1) * CHUNK, :] for g in range(n_chunks)]
    b_last_rows = jnp.concatenate([jnp.broadcast_to(b, (CHUNK, b.shape[1])) for b in b_last], axis=0)
    ke = (k * jnp.exp(b_last_rows - bcum)).astype(bf16)
    decay = [jnp.exp(b) for b in b_last]
    same_chunk_causal = (r // CHUNK == c // CHUNK) & (c <= r)
    nt = (((1,), (1,)), ((), ()))

    def block_diag(xh):
        blocks = []
        for g in range(n_chunks):
            parts = []
            if g > 0:
                parts.append(jnp.zeros((g * CHUNK, xh.shape[1]), xh.dtype))
            parts.append(xh[g * CHUNK:(g + 1) * CHUNK, :])
            if g < n_chunks - 1:
                parts.append(jnp.zeros((tg - (g + 1) * CHUNK, xh.shape[1]), xh.dtype))
            blocks.append(jnp.concatenate(parts, axis=0))
        return jnp.concatenate(blocks, axis=1)

    heads = [(h, slice(h * GLA_DK, (h + 1) * GLA_DK), slice(h * GLA_DV, (h + 1) * GLA_DV))
             for h in range(GLA_HEADS)]
    att = [lax.dot_general(qd[:, ks], kd[:, ks], nt, preferred_element_type=f32) for h, ks, vs in heads]
    ut_all = [lax.dot_general(v_ref[:, vs], block_diag(ke[:, ks]), (((0,), (0,)), ((), ())),
                              preferred_element_type=f32) for h, ks, vs in heads]
    st_cat = []
    for h, ks, vs in heads:
        st = st_ref[h]
        st_in = []
        for g in range(n_chunks):
            st_in.append(st.astype(bf16))
            st = st * decay[g][:, ks] + ut_all[h][:, g * GLA_DK:(g + 1) * GLA_DK]
        st_ref[h] = st
        st_cat.append(jnp.concatenate(st_in, axis=1))
    for h, ks, vs in heads:
        att_h = jnp.where(same_chunk_causal, att[h], 0.0).astype(bf16)
        o = (jnp.dot(att_h, v_ref[:, vs], preferred_element_type=f32)
             + lax.dot_general(block_diag(qd[:, ks]), st_cat[h], nt, preferred_element_type=f32))
        on = _rms(o, ng_ref[:, vs])
        o_ref[:, vs] = (on * _silu(z_ref[:, vs].astype(f32))).astype(o_ref.dtype)


def _pool_gla_kernel(u_ref, zb_ref, bm_ref, bh_ref, pw_ref, ps_ref,
                     q_ref, k_ref, v_ref, glr_ref, zc_ref, wg_ref, bg_ref, ng_ref,
                     yb_ref, yc_ref, halo_ref, st_ref, *, tg):
    @pl.when(pl.program_id(1) == 0)
    def _():
        halo_ref[...] = jnp.zeros(halo_ref.shape, bf16)
        st_ref[...] = jnp.zeros(st_ref.shape, f32)

    _pool_body(u_ref, zb_ref, bm_ref, bh_ref, pw_ref, ps_ref, yb_ref, halo_ref, tm=tg)
    _gla_body(q_ref, k_ref, v_ref, glr_ref, zc_ref, wg_ref, bg_ref, ng_ref, yc_ref, st_ref, tg=tg)


def _pool_gla(proj, pool_w, pool_scale, wg, bg, ng, B, S, tg):
    nb = S // tg
    T = B * S
    W = BRANCH_WIDTH
    QK = GLA_HEADS * GLA_DK
    band_main, band_halo = _pool_bands(tg)
    rows = lambda b, j: b * nb + j
    tile = lambda width, col: pl.BlockSpec((tg, width), lambda b, j: (rows(b, j), col // width))
    const = lambda a: pl.BlockSpec(a.shape, lambda b, j: (0,) * a.ndim)
    out = pl.BlockSpec((tg, W), lambda b, j: (rows(b, j), 0))
    return pl.pallas_call(
        functools.partial(_pool_gla_kernel, tg=tg),
        grid=(B, nb),
        in_specs=[
            tile(W, P_UB), tile(W, P_ZB), const(band_main), const(band_halo), const(pool_w), const(pool_scale),
            tile(QK, P_QC), tile(QK, P_KC), tile(W, P_VC), tile(128, P_GLR), tile(W, P_ZC),
            const(wg), const(bg), const(ng),
        ],
        out_specs=[out, out],
        out_shape=[jax.ShapeDtypeStruct((T, W), bf16)] * 2,
        scratch_shapes=[pltpu.VMEM((POOL_HALO, W), bf16),
                        pltpu.VMEM((GLA_HEADS, GLA_DV, GLA_DK), f32)],
        compiler_params=_cparams(2),
        name="pool_gla",
    )(proj, proj, band_main, band_halo, pool_w, pool_scale,
      proj, proj, proj, proj, proj, wg, bg, ng)


def _merge_kernel(ya_ref, yb_ref, yc_ref, g0_ref, g1_ref, g2_ref, bm_ref, wb_ref, wo_ref,
                  x_ref, ng_ref, *out_refs, final):
    merged = None
    for n, (y_ref, g_ref) in enumerate(((ya_ref, g0_ref), (yb_ref, g1_ref), (yc_ref, g2_ref))):
        p = jnp.dot(y_ref[...], wb_ref[n], preferred_element_type=f32)
        gate = 1.0 / (1.0 + jnp.exp(-(g_ref[...].astype(f32) + bm_ref[n:n + 1, :])))
        merged = gate * p if merged is None else merged + gate * p
    xn = x_ref[...] + jnp.dot(merged.astype(bf16), wo_ref[...], preferred_element_type=f32)
    hn = _rms(xn, ng_ref[...])
    if final:
        out_refs[0][...] = hn
    else:
        out_refs[0][...] = xn
        out_refs[1][...] = hn.astype(bf16)


def _merge(ya, yb, yc, proj, bm, wb, wo, x, ng, tm, final):
    T, D = x.shape
    W = BRANCH_WIDTH
    row = lambda i: (i, 0)
    ytile = pl.BlockSpec((tm, W), row)
    xtile = pl.BlockSpec((tm, D), row)
    resident = dict(pipeline_mode=pl.Buffered(1))
    if final:
        out_specs = [xtile]
        out_shape = [jax.ShapeDtypeStruct((T, D), f32)]
    else:
        out_specs = [xtile, xtile]
        out_shape = [jax.ShapeDtypeStruct((T, D), f32), jax.ShapeDtypeStruct((T, D), bf16)]
    return pl.pallas_call(
        functools.partial(_merge_kernel, final=final),
        grid=(T // tm,),
        in_specs=[
            ytile, ytile, ytile,
            pl.BlockSpec((tm, D), lambda i: (i, P_GATES // D + 0)),
            pl.BlockSpec((tm, D), lambda i: (i, P_GATES // D + 1)),
            pl.BlockSpec((tm, D), lambda i: (i, P_GATES // D + 2)),
            pl.BlockSpec((N_BRANCH, D), lambda i: (0, 0)),
            pl.BlockSpec(wb.shape, lambda i: (0, 0, 0), **resident),
            pl.BlockSpec(wo.shape, lambda i: (0, 0), **resident),
            xtile,
            pl.BlockSpec((1, D), lambda i: (0, 0)),
        ],
        out_specs=out_specs,
        out_shape=out_shape,
        compiler_params=_cparams(1),
        name="merge_final" if final else "merge",
    )(ya, yb, yc, proj, proj, proj, bm, wb, wo, x, ng)


_W_IN_SLABS = (
    (P_GATES, _O_GATES, N_BRANCH * D_MODEL), (P_ZA, _O_ZA, 1024), (P_UB, _O_UB, 1024),
    (P_ZB, _O_ZB, 1024), (P_VC, _O_VC, 1024), (P_ZC, _O_ZC, 1024), (P_CQ, _O_CQ, 512),
    (P_CKV, _O_CKV, 512), (P_QC, _O_QC, 512), (P_KC, _O_KC, 512),
)


def _permute_w_in_kernel(w_ref, o_ref):
    for dst, src, n in _W_IN_SLABS:
        o_ref[0, dst:dst + n, :] = w_ref[0, src:src + n, :].astype(bf16)
    cols = w_ref.shape[2]
    half = MLA_ROPE // 2
    zero = lambda n: jnp.zeros((n, cols), bf16)
    for part, src in enumerate((_O_KR, _O_KR + half)):
        o_ref[0, P_KR + 2 * part * half:P_KR + (2 * part + 1) * half, :] = w_ref[0, src:src + half, :].astype(bf16)
        o_ref[0, P_KR + (2 * part + 1) * half:P_KR + (2 * part + 2) * half, :] = zero(half)
    o_ref[0, P_GLR:P_GLR + GLA_GATE_RANK, :] = w_ref[0, _O_GLR:_O_GLR + GLA_GATE_RANK, :].astype(bf16)
    o_ref[0, P_GLR + GLA_GATE_RANK:P_TOTAL, :] = zero(P_TOTAL - P_GLR - GLA_GATE_RANK)


def _permute_w_in(w, tc=128):
    L, D, N = w.shape
    return pl.pallas_call(
        _permute_w_in_kernel,
        grid=(L, D // tc),
        in_specs=[pl.BlockSpec((1, N, tc), lambda l, i: (l, 0, i))],
        out_specs=pl.BlockSpec((1, P_TOTAL, tc), lambda l, i: (l, 0, i)),
        out_shape=jax.ShapeDtypeStruct((L, P_TOTAL, D), bf16),
        compiler_params=_cparams(2),
        name="permute_w_in",
    )(jnp.swapaxes(w, 1, 2))


def _permute_w_uq(w):
    half = MLA_ROPE // 2
    w = w.reshape(MLA_LORA, MLA_HEADS, MLA_NOPE + MLA_ROPE)
    z = jnp.zeros((MLA_LORA, MLA_HEADS, half), w.dtype)
    out = jnp.concatenate([w[..., :MLA_NOPE], w[..., MLA_NOPE:MLA_NOPE + half], z,
                           w[..., MLA_NOPE + half:], z], axis=-1)
    return out.reshape(MLA_LORA, MLA_HEADS * HEAD_PAD).astype(bf16)


def _split_w_ukv(w):
    w = w.reshape(MLA_LORA, MLA_HEADS, MLA_NOPE + MLA_V)
    wuk = w[..., :MLA_NOPE].reshape(MLA_LORA, MLA_HEADS * MLA_NOPE).astype(bf16)
    wuvt = w[..., MLA_NOPE:].reshape(MLA_LORA, MLA_HEADS * MLA_V).T.astype(bf16)
    return wuk, wuvt


def _pick(n, pref):
    return pref if n % pref == 0 else n


def kernel(x, positions, norm_g, w_in, b_merge, mla_q_norm, mla_kv_norm, mla_w_uq, mla_w_ukv,
           pool_w, pool_scale, gla_w_gate, gla_b_gate, gla_norm, w_branch, w_out, final_norm):
    B, S, D = x.shape
    assert D == D_MODEL and S % 2048 == 0
    T = B * S
    depth = norm_g.shape[0]
    blk = 512
    tq_attn = 2048
    tm_proj = _pick(T, 1024)
    tn_proj = 2304
    tg = 256
    tm_merge = 256

    xf = x.reshape(T, D)
    cos_t, sin_t = _rope_tables(positions.reshape(T, 1), 512)
    h = _rmsnorm(xf, norm_g[0][None, :], 512)
    out = None
    w_in_p = _permute_w_in(w_in)
    for l in range(depth):
        wuq = _permute_w_uq(mla_w_uq[l])
        wuk, wuvt = _split_w_ukv(mla_w_ukv[l])
        wg = jnp.zeros((128, GLA_HEADS * GLA_DK), bf16).at[:GLA_GATE_RANK].set(
            gla_w_gate[l].astype(bf16))

        proj = _inproj(h, w_in_p, l, tm_proj, tn_proj)
        q, k, vt = _mla_up(proj, cos_t, sin_t, mla_q_norm[l][None, :], mla_kv_norm[l][None, :],
                           wuq, wuk, wuvt, B, S, blk)
        ya = _attention(q, k, vt, proj, B, S, tq_attn, blk)
        yb, yc = _pool_gla(proj, pool_w[l].astype(bf16), pool_scale[l][None, :],
                           wg, gla_b_gate[l][None, :], gla_norm[l][None, :], B, S, tg)

        final = l == depth - 1
        next_g = final_norm if final else norm_g[l + 1]
        res = _merge(ya, yb, yc, proj, b_merge[l], w_branch[l].astype(bf16),
                     w_out[l].astype(bf16), xf, next_g[None, :], tm_merge, final)
        if final:
            out = res[0]
        else:
            xf, h = res
    return out.reshape(B, S, D)
```

```python
import functools
import math

import jax
import jax.numpy as jnp
import numpy as np
from jax import lax
from jax.experimental import pallas as pl
from jax.experimental.pallas import tpu as pltpu

f32 = jnp.float32
bf16 = jnp.bfloat16

D_MODEL = 2048
CHUNK = 64
EPS = 1e-6

MLA_HEADS = 8
MLA_NOPE = 128
MLA_ROPE = 64
MLA_V = 128
VT_ROWS = MLA_V + 16
FAST_MAX_JUMP = 24.0
MLA_LORA = 512
ROPE_THETA = 10000.0
HEAD_PAD = 256

POOL_WINDOWS = (2, 4, 8, 16)
POOL_GROUP_W = 256
POOL_HALO = 16

GLA_HEADS = 4
GLA_DK = 128
GLA_DV = 256
GLA_GATE_RANK = 16
GLA_GATE_TAU = 16.0

N_BRANCH = 3
BRANCH_WIDTH = 1024

_O_CQ, _O_CKV, _O_KR, _O_ZA, _O_UB, _O_ZB = 0, 512, 1024, 1088, 2112, 3136
_O_QC, _O_KC, _O_VC, _O_GLR, _O_ZC, _O_GATES = 4160, 4672, 5184, 6208, 6224, 7248

P_GATES, P_ZA, P_UB, P_ZB, P_VC, P_ZC = 0, 6144, 7168, 8192, 9216, 10240
P_CQ, P_CKV, P_QC, P_KC, P_KR, P_GLR = 11264, 11776, 12288, 12800, 13312, 13440
P_TOTAL = 13824

VMEM_LIMIT = 56 * 1024 * 1024


def _cparams(n_axes, flags=None):
    return pltpu.CompilerParams(
        dimension_semantics=("arbitrary",) * n_axes, vmem_limit_bytes=VMEM_LIMIT, flags=flags)


def _silu(z):
    return z * (1.0 / (1.0 + jnp.exp(-z)))


def _rms(x, g):
    return x * lax.rsqrt(jnp.mean(x * x, axis=-1, keepdims=True) + EPS) * g


def _prologue_kernel(x_ref, g_ref, pos_ref, inv_ref, sign_ref, h_ref, cos_ref, sin_ref):
    h_ref[...] = _rms(x_ref[...], g_ref[...]).astype(h_ref.dtype)
    ang = pos_ref[...].astype(f32) * inv_ref[...]
    cos_ref[...] = jnp.cos(ang)
    sin_ref[...] = jnp.sin(ang) * sign_ref[...]


def _prologue(x, g, pos, tm):
    T, D = x.shape
    half = MLA_ROPE // 2
    inv = 1.0 / (ROPE_THETA ** (jnp.arange(0, MLA_ROPE, 2, dtype=f32) / MLA_ROPE))
    zero = jnp.zeros((half,), f32)
    one = jnp.ones((half,), f32)
    inv_row = jnp.concatenate([inv, zero, inv, zero])[None, :]
    sign_row = jnp.concatenate([-one, zero, one, zero])[None, :]
    row = pl.BlockSpec((1, 128), lambda i: (0, 0))
    tab = pl.BlockSpec((tm, 128), lambda i: (i, 0))
    return pl.pallas_call(
        _prologue_kernel,
        grid=(T // tm,),
        in_specs=[pl.BlockSpec((tm, D), lambda i: (i, 0)),
                  pl.BlockSpec((1, D), lambda i: (0, 0)),
                  pl.BlockSpec((tm, 1), lambda i: (i, 0)), row, row],
        out_specs=[pl.BlockSpec((tm, D), lambda i: (i, 0)), tab, tab],
        out_shape=[jax.ShapeDtypeStruct((T, D), bf16)] + [jax.ShapeDtypeStruct((T, 128), f32)] * 2,
        compiler_params=_cparams(1),
        name="prologue",
    )(x, g, pos, inv_row, sign_row)


def _inproj_kernel(h_ref, wt_ref, o_ref):
    o_ref[...] = lax.dot_general(h_ref[...], wt_ref[...], (((1,), (1,)), ((), ())),
                                 preferred_element_type=f32).astype(o_ref.dtype)


def _inproj(h, wt, layer, tm, tn):
    T, D = h.shape
    N = wt.shape[1]
    return pl.pallas_call(
        _inproj_kernel,
        grid=(T // tm, N // tn),
        in_specs=[pl.BlockSpec((tm, D), lambda i, j: (i, 0)),
                  pl.BlockSpec((None, tn, D), lambda i, j: (layer, j, 0))],
        out_specs=pl.BlockSpec((tm, tn), lambda i, j: (i, j)),
        out_shape=jax.ShapeDtypeStruct((T, N), bf16),
        compiler_params=_cparams(2),
        name="inproj",
    )(h, wt)


def _mla_up_kernel(cq_ref, ckv_ref, kr_ref, cos_ref, sin_ref, gq_ref, gkv_ref,
                   wuq_ref, wuk_ref, wuvt_ref, q_ref, k_ref, vt_ref, *, qscale):
    cos = cos_ref[...]
    sin = sin_ref[...]

    def rope(x):
        return x * cos + pltpu.roll(x, 64, 1) * sin

    cqn = _rms(cq_ref[...].astype(f32), gq_ref[...]).astype(bf16)
    qf = jnp.dot(cqn, wuq_ref[...], preferred_element_type=f32) * qscale
    for h in range(MLA_HEADS):
        c0 = h * HEAD_PAD
        q_ref[0, h, 0, :, 0:128] = qf[:, c0:c0 + 128].astype(bf16)
        q_ref[0, h, 0, :, 128:256] = rope(qf[:, c0 + 128:c0 + 256]).astype(bf16)

    ckvn = _rms(ckv_ref[...].astype(f32), gkv_ref[...]).astype(bf16)
    kn = jnp.dot(ckvn, wuk_ref[...], preferred_element_type=f32)
    krr = rope(kr_ref[...].astype(f32)).astype(bf16)
    for h in range(MLA_HEADS):
        k_ref[0, h, 0, :, 0:128] = kn[:, h * 128:(h + 1) * 128].astype(bf16)
        k_ref[0, h, 0, :, 128:256] = krr

    vt = lax.dot_general(wuvt_ref[...], ckvn, (((1,), (1,)), ((), ())),
                         preferred_element_type=f32)
    ones = jnp.ones((VT_ROWS - MLA_V, vt.shape[1]), bf16)
    for h in range(MLA_HEADS):
        vt_ref[0, h, 0, 0:MLA_V, :] = vt[h * MLA_V:(h + 1) * MLA_V, :].astype(bf16)
        vt_ref[0, h, 0, MLA_V:VT_ROWS, :] = ones


def _mla_up(proj, cos_t, sin_t, gq, gkv, wuq, wuk, wuvt, B, S, blk):
    nb = S // blk
    H = MLA_HEADS

    def rows(b, j):
        return b * nb + j

    const = lambda b, j: (0, 0)
    qk_shape = jax.ShapeDtypeStruct((B, H, nb, blk, HEAD_PAD), bf16)
    qk_spec = pl.BlockSpec((1, H, 1, blk, HEAD_PAD), lambda b, j: (b, 0, j, 0, 0))
    qscale = (MLA_NOPE + MLA_ROPE) ** -0.5 * math.log2(math.e)
    return pl.pallas_call(
        functools.partial(_mla_up_kernel, qscale=qscale),
        grid=(B, nb),
        in_specs=[
            pl.BlockSpec((blk, 512), lambda b, j: (rows(b, j), P_CQ // 512)),
            pl.BlockSpec((blk, 512), lambda b, j: (rows(b, j), P_CKV // 512)),
            pl.BlockSpec((blk, 128), lambda b, j: (rows(b, j), P_KR // 128)),
            pl.BlockSpec((blk, 128), lambda b, j: (rows(b, j), 0)),
            pl.BlockSpec((blk, 128), lambda b, j: (rows(b, j), 0)),
            pl.BlockSpec((1, MLA_LORA), const),
            pl.BlockSpec((1, MLA_LORA), const),
            pl.BlockSpec(wuq.shape, const),
            pl.BlockSpec(wuk.shape, const),
            pl.BlockSpec(wuvt.shape, const),
        ],
        out_specs=[qk_spec, qk_spec,
                   pl.BlockSpec((1, H, 1, VT_ROWS, blk), lambda b, j: (b, 0, j, 0, 0))],
        out_shape=[qk_shape, qk_shape,
                   jax.ShapeDtypeStruct((B, H, nb, VT_ROWS, blk), bf16)],
        compiler_params=_cparams(2),
        name="mla_up",
    )(proj, proj, proj, cos_t, sin_t, gq, gkv, wuq, wuk, wuvt)


def _attn_kernel(q_ref, k_ref, vt_ref, z_ref, o_ref,
                 m_ref, r_ref, risk_ref, al0_ref, al1_ref, mt0_ref, mt1_ref,
                 acc_ref, accd_ref, md_ref, s0_ref, s1_ref, p0_ref, p1_ref, *, tq, tk):
    i = pl.program_id(2)
    kpq = tq // tk
    assert kpq % 2 == 0
    n_full = i * kpq
    nt = (((1,), (1,)), ((), ()))
    m_ref[...] = jnp.full(m_ref.shape, -1e30, f32)
    acc_ref[...] = jnp.zeros(acc_ref.shape, f32)

    def scores(j, cols):
        return lax.dot_general(k_ref[0, 0, j], q_ref[0, 0, 0, cols, :], nt,
                               preferred_element_type=f32)

    def put_scores(s, bufs, cols):
        s_ref, mt_ref, _ = bufs
        s_ref[:, cols] = s
        mt_ref[:, cols] = jnp.max(s, axis=0, keepdims=True)

    def softmax_pv(j, bufs, cols):
        s_ref, mt_ref, al_ref = bufs
        m_old = m_ref[:, cols]
        m_new = jnp.maximum(m_old, mt_ref[:, cols])
        al_ref[:, cols] = jnp.exp2(m_old - m_new)
        m_ref[:, cols] = m_new
        p = jnp.exp2(s_ref[:, cols] - m_new).astype(bf16)
        acc_ref[:, cols] = (al_ref[:, cols] * acc_ref[:, cols]
                            + jnp.dot(vt_ref[0, 0, j], p, preferred_element_type=f32))

    everything = slice(0, tq)
    buf0, buf1 = (s0_ref, mt0_ref, al0_ref), (s1_ref, mt1_ref, al1_ref)

    def put_diag_scores(kt):
        cols = slice(kt * tk, tq)
        s = scores(n_full + kt, cols)
        kc = lax.broadcasted_iota(jnp.int32, s.shape, 0) // CHUNK
        qc = lax.broadcasted_iota(jnp.int32, s.shape, 1) // CHUNK
        put_scores(jnp.where(kc <= qc, s, -1e30), (buf0, buf1)[kt % 2], cols)

    put_diag_scores(0)
    for kt in range(kpq):
        if kt + 1 < kpq:
            put_diag_scores(kt + 1)
        softmax_pv(n_full + kt, (buf0, buf1)[kt % 2], slice(kt * tk, tq))

    accd_ref[...] = acc_ref[...]
    md_ref[...] = m_ref[...]
    r_ref[...] = m_ref[...]
    risk_ref[...] = jnp.zeros(risk_ref.shape, f32)
    p1_ref[...] = jnp.zeros(p1_ref.shape, bf16)
    al1_ref[...] = jnp.ones(al1_ref.shape, f32)
    fast = ((p0_ref, al0_ref), (p1_ref, al1_ref))

    def fast_pv(j, slot):
        p_ref, al_ref = fast[slot]
        acc_ref[...] = al_ref[...] * acc_ref[...] + jnp.dot(vt_ref[0, 0, j], p_ref[...],
                                                             preferred_element_type=f32)

    def fast_step(j, slot):
        p_ref, al_ref = fast[slot]
        ref = m_ref[...]
        s = scores(j, everything)
        mt = jnp.max(s, axis=0, keepdims=True)
        p_ref[...] = jnp.exp2(s - ref).astype(bf16)
        al_ref[...] = jnp.exp2(r_ref[...] - ref)
        r_ref[...] = ref
        risk_ref[...] = jnp.maximum(risk_ref[...], mt - ref)
        m_ref[...] = jnp.maximum(ref, mt)
        fast_pv(jnp.maximum(j - 1, 0), 1 - slot)

    def fast_body(t, carry):
        for u in range(kpq):
            fast_step(kpq * t + u, u % 2)
        return carry

    lax.fori_loop(0, i, fast_body, 0)
    fast_pv(jnp.maximum(n_full - 1, 0), 1)

    @pl.when(jnp.max(risk_ref[...]) > FAST_MAX_JUMP)
    def _():
        acc_ref[...] = accd_ref[...]
        m_ref[...] = md_ref[...]
        put_scores(scores(0, everything), buf0, everything)

        def body(t, carry):
            for u in range(kpq):
                j = kpq * t + u
                put_scores(scores(j + 1, everything), (buf0, buf1)[(u + 1) % 2], everything)
                softmax_pv(j, (buf0, buf1)[u % 2], everything)
            return carry

        lax.fori_loop(0, i, body, 0)

    o = acc_ref[0:MLA_V, :] / acc_ref[MLA_V:MLA_V + 1, :]
    z = z_ref[...].astype(f32)
    o_ref[...] = (o.T * _silu(z)).astype(o_ref.dtype)


def _attention(q, k, vt, proj, B, S, tq, tk):
    H = MLA_HEADS
    nq, nk = S // tq, S // tk
    T = B * S
    q = q.reshape(B, H, nq, tq, HEAD_PAD)
    row = pltpu.VMEM((1, tq), f32)
    acc = pltpu.VMEM((VT_ROWS, tq), f32)
    return pl.pallas_call(
        functools.partial(_attn_kernel, tq=tq, tk=tk),
        grid=(B, H, nq),
        in_specs=[
            pl.BlockSpec((1, 1, 1, tq, HEAD_PAD), lambda b, h, i: (b, h, i, 0, 0)),
            pl.BlockSpec((1, 1, nk, tk, HEAD_PAD), lambda b, h, i: (b, h, 0, 0, 0)),
            pl.BlockSpec((1, 1, nk, VT_ROWS, tk), lambda b, h, i: (b, h, 0, 0, 0)),
            pl.BlockSpec((tq, MLA_V), lambda b, h, i: (b * nq + i, P_ZA // MLA_V + h)),
        ],
        out_specs=pl.BlockSpec((tq, MLA_V), lambda b, h, i: (b * nq + i, h)),
        out_shape=jax.ShapeDtypeStruct((T, BRANCH_WIDTH), bf16),
        scratch_shapes=[row] * 7 + [acc, acc, row,
                                    pltpu.VMEM((tk, tq), f32), pltpu.VMEM((tk, tq), f32),
                                    pltpu.VMEM((tk, tq), bf16), pltpu.VMEM((tk, tq), bf16)],
        compiler_params=_cparams(3),
        name="mla_attention",
    )(q, k, vt, proj)


def _pool_body(u_ref, z_ref, bm_ref, bh_ref, w_ref, sc_ref, o_ref, halo_ref, *, tm):
    j = pl.program_id(1)
    H = POOL_HALO
    u = u_ref[...]
    halo = halo_ref[...]
    t = j * tm + lax.broadcasted_iota(jnp.int32, (tm, 1), 0)
    groups = [(g, w, slice(g * POOL_GROUP_W, (g + 1) * POOL_GROUP_W)) for g, w in enumerate(POOL_WINDOWS)]
    sums = [jnp.dot(bm_ref[g], u[:, cs], preferred_element_type=f32)
            + jnp.dot(bh_ref[g], halo[:, cs], preferred_element_type=f32) for g, w, cs in groups]
    means = [(sums[g] / jnp.minimum(t + 1, w).astype(f32) - u[:, cs].astype(f32)).astype(bf16)
             for g, w, cs in groups]
    yield
    ys = [jnp.dot(means[g], w_ref[g], preferred_element_type=f32) for g, w, cs in groups]
    for g, w, cs in groups:
        o_ref[:, cs] = (ys[g] * sc_ref[:, cs] * _silu(z_ref[:, cs].astype(f32))).astype(o_ref.dtype)
    halo_ref[...] = u[tm - H:tm, :]


def _pool_bands(tm):
    t = np.arange(tm)[:, None]
    main = np.stack([((t - np.arange(tm)[None, :] >= 0) & (t - np.arange(tm)[None, :] < w))
                     for w in POOL_WINDOWS])
    halo = np.stack([(t + POOL_HALO - np.arange(POOL_HALO)[None, :] < w) for w in POOL_WINDOWS])
    return jnp.asarray(main, bf16), jnp.asarray(halo, bf16)


def _gla_body(q_ref, k_ref, v_ref, glr_ref, z_ref, wg_ref, bg_ref, ng_ref, o_ref,
              st_ref, *, tg):
    x = jnp.dot(glr_ref[...], wg_ref[...], preferred_element_type=f32) + bg_ref[...]
    log_a = -(jnp.maximum(-x, 0.0) + jnp.log(1.0 + jnp.exp(-jnp.abs(x)))) / GLA_GATE_TAU

    r = lax.broadcasted_iota(jnp.int32, (tg, tg), 0)
    c = lax.broadcasted_iota(jnp.int32, (tg, tg), 1)
    tri = ((r // CHUNK == c // CHUNK) & (c <= r)).astype(bf16)
    hi = log_a.astype(bf16)
    rem = log_a - hi.astype(f32)
    mid = rem.astype(bf16)
    lo = (rem - mid.astype(f32)).astype(bf16)
    yield
    bcum = (jnp.dot(tri, hi, preferred_element_type=f32)
            + jnp.dot(tri, mid, preferred_element_type=f32)
            + jnp.dot(tri, lo, preferred_element_type=f32))

    n_chunks = tg // CHUNK
    q = q_ref[...].astype(f32) * (GLA_DK ** -0.5)
    k = k_ref[...].astype(f32)
    qd = (q * jnp.exp(bcum)).astype(bf16)
    kd = (k * jnp.exp(-bcum)).astype(bf16)
    b_last = [bcum[(g + 1) * CHUNK - 1:(g + 1) * CHUNK, :] for g in range(n_chunks)]
    b_last_rows = jnp.concatenate([jnp.broadcast_to(b, (CHUNK, b.shape[1])) for b in b_last], axis=0)
    ke = (k * jnp.exp(b_last_rows - bcum)).astype(bf16)
    decay = [jnp.exp(b) for b in b_last]
    same_chunk_causal = (r // CHUNK == c // CHUNK) & (c <= r)
    nt = (((1,), (1,)), ((), ()))

    def block_diag(xh):
        blocks = []
        for g in range(n_chunks):
            parts = []
            if g > 0:
                parts.append(jnp.zeros((g * CHUNK, xh.shape[1]), xh.dtype))
            parts.append(xh[g * CHUNK:(g + 1) * CHUNK, :])
            if g < n_chunks - 1:
                parts.append(jnp.zeros((tg - (g + 1) * CHUNK, xh.shape[1]), xh.dtype))
            blocks.append(jnp.concatenate(parts, axis=0))
        return jnp.concatenate(blocks, axis=1)

    heads = [(h, slice(h * GLA_DK, (h + 1) * GLA_DK), slice(h * GLA_DV, (h + 1) * GLA_DV))
             for h in range(GLA_HEADS)]
    att = [lax.dot_general(qd[:, ks], kd[:, ks], nt, preferred_element_type=f32) for h, ks, vs in heads]
    ut_all = [lax.dot_general(v_ref[:, vs], block_diag(ke[:, ks]), (((0,), (0,)), ((), ())),
                              preferred_element_type=f32) for h, ks, vs in heads]
    st_cat = []
    for h, ks, vs in heads:
        st = st_ref[h]
        st_in = []
        for g in range(n_chunks):
            st_in.append(st.astype(bf16))
            st = st * decay[g][:, ks] + ut_all[h][:, g * GLA_DK:(g + 1) * GLA_DK]
        st_ref[h] = st
        st_cat.append(jnp.concatenate(st_in, axis=1))
    for h, ks, vs in heads:
        att_h = jnp.where(same_chunk_causal, att[h], 0.0).astype(bf16)
        o = (jnp.dot(att_h, v_ref[:, vs], preferred_element_type=f32)
             + lax.dot_general(block_diag(qd[:, ks]), st_cat[h], nt, preferred_element_type=f32))
        on = _rms(o, ng_ref[:, vs])
        o_ref[:, vs] = (on * _silu(z_ref[:, vs].astype(f32))).astype(o_ref.dtype)


def _pool_gla_kernel(u_ref, zb_ref, bm_ref, bh_ref, pw_ref, ps_ref,
                     q_ref, k_ref, v_ref, glr_ref, zc_ref, wg_ref, bg_ref, ng_ref,
                     yb_ref, yc_ref, halo_ref, st_ref, *, tg):
    @pl.when(pl.program_id(1) == 0)
    def _():
        halo_ref[...] = jnp.zeros(halo_ref.shape, bf16)
        st_ref[...] = jnp.zeros(st_ref.shape, f32)

    pool = _pool_body(u_ref, zb_ref, bm_ref, bh_ref, pw_ref, ps_ref, yb_ref, halo_ref, tm=tg)
    gla = _gla_body(q_ref, k_ref, v_ref, glr_ref, zc_ref, wg_ref, bg_ref, ng_ref, yc_ref, st_ref, tg=tg)
    next(pool)
    next(gla)
    for _ in pool:
        pass
    for _ in gla:
        pass


def _pool_gla(proj, pool_w, pool_scale, wg, bg, ng, B, S, tg):
    nb = S // tg
    T = B * S
    W = BRANCH_WIDTH
    QK = GLA_HEADS * GLA_DK
    band_main, band_halo = _pool_bands(tg)
    rows = lambda b, j: b * nb + j
    tile = lambda width, col: pl.BlockSpec((tg, width), lambda b, j: (rows(b, j), col // width))
    const = lambda a: pl.BlockSpec(a.shape, lambda b, j: (0,) * a.ndim)
    out = pl.BlockSpec((tg, W), lambda b, j: (rows(b, j), 0))
    return pl.pallas_call(
        functools.partial(_pool_gla_kernel, tg=tg),
        grid=(B, nb),
        in_specs=[
            tile(W, P_UB), tile(W, P_ZB), const(band_main), const(band_halo), const(pool_w), const(pool_scale),
            tile(QK, P_QC), tile(QK, P_KC), tile(W, P_VC), tile(128, P_GLR), tile(W, P_ZC),
            const(wg), const(bg), const(ng),
        ],
        out_specs=[out, out],
        out_shape=[jax.ShapeDtypeStruct((T, W), bf16)] * 2,
        scratch_shapes=[pltpu.VMEM((POOL_HALO, W), bf16),
                        pltpu.VMEM((GLA_HEADS, GLA_DV, GLA_DK), f32)],
        compiler_params=_cparams(2),
        name="pool_gla",
    )(proj, proj, band_main, band_halo, pool_w, pool_scale,
      proj, proj, proj, proj, proj, wg, bg, ng)


def _merge_kernel(ya_ref, yb_ref, yc_ref, g0_ref, g1_ref, g2_ref, bm_ref, wb_ref, wo_ref,
                  x_ref, ng_ref, *out_refs, final):
    merged = None
    for n, (y_ref, g_ref) in enumerate(((ya_ref, g0_ref), (yb_ref, g1_ref), (yc_ref, g2_ref))):
        p = jnp.dot(y_ref[...], wb_ref[n], preferred_element_type=f32)
        gate = 1.0 / (1.0 + jnp.exp(-(g_ref[...].astype(f32) + bm_ref[n:n + 1, :])))
        merged = gate * p if merged is None else merged + gate * p
    xn = x_ref[...] + jnp.dot(merged.astype(bf16), wo_ref[...], preferred_element_type=f32)
    hn = _rms(xn, ng_ref[...])
    if final:
        out_refs[0][...] = hn
    else:
        out_refs[0][...] = xn
        out_refs[1][...] = hn.astype(bf16)


def _merge(ya, yb, yc, proj, bm, wb, wo, x, ng, tm, final):
    T, D = x.shape
    W = BRANCH_WIDTH
    row = lambda i: (i, 0)
    ytile = pl.BlockSpec((tm, W), row)
    xtile = pl.BlockSpec((tm, D), row)
    resident = dict(pipeline_mode=pl.Buffered(1))
    if final:
        out_specs = [xtile]
        out_shape = [jax.ShapeDtypeStruct((T, D), f32)]
    else:
        out_specs = [xtile, xtile]
        out_shape = [jax.ShapeDtypeStruct((T, D), f32), jax.ShapeDtypeStruct((T, D), bf16)]
    return pl.pallas_call(
        functools.partial(_merge_kernel, final=final),
        grid=(T // tm,),
        in_specs=[
            ytile, ytile, ytile,
            pl.BlockSpec((tm, D), lambda i: (i, P_GATES // D + 0)),
            pl.BlockSpec((tm, D), lambda i: (i, P_GATES // D + 1)),
            pl.BlockSpec((tm, D), lambda i: (i, P_GATES // D + 2)),
            pl.BlockSpec((N_BRANCH, D), lambda i: (0, 0)),
            pl.BlockSpec(wb.shape, lambda i: (0, 0, 0), **resident),
            pl.BlockSpec(wo.shape, lambda i: (0, 0), **resident),
            xtile,
            pl.BlockSpec((1, D), lambda i: (0, 0)),
        ],
        out_specs=out_specs,
        out_shape=out_shape,
        compiler_params=_cparams(1),
        name="merge_final" if final else "merge",
    )(ya, yb, yc, proj, proj, proj, bm, wb, wo, x, ng)


_W_IN_SLABS = (
    (P_GATES, _O_GATES, N_BRANCH * D_MODEL), (P_ZA, _O_ZA, 1024), (P_UB, _O_UB, 1024),
    (P_ZB, _O_ZB, 1024), (P_VC, _O_VC, 1024), (P_ZC, _O_ZC, 1024), (P_CQ, _O_CQ, 512),
    (P_CKV, _O_CKV, 512), (P_QC, _O_QC, 512), (P_KC, _O_KC, 512),
)


def _permute_w_in_kernel(w_ref, o_ref):
    for dst, src, n in _W_IN_SLABS:
        o_ref[0, dst:dst + n, :] = w_ref[0, src:src + n, :].astype(bf16)
    cols = w_ref.shape[2]
    half = MLA_ROPE // 2
    zero = lambda n: jnp.zeros((n, cols), bf16)
    for part, src in enumerate((_O_KR, _O_KR + half)):
        o_ref[0, P_KR + 2 * part * half:P_KR + (2 * part + 1) * half, :] = w_ref[0, src:src + half, :].astype(bf16)
        o_ref[0, P_KR + (2 * part + 1) * half:P_KR + (2 * part + 2) * half, :] = zero(half)
    o_ref[0, P_GLR:P_GLR + GLA_GATE_RANK, :] = w_ref[0, _O_GLR:_O_GLR + GLA_GATE_RANK, :].astype(bf16)
    o_ref[0, P_GLR + GLA_GATE_RANK:P_TOTAL, :] = zero(P_TOTAL - P_GLR - GLA_GATE_RANK)


def _permute_w_in(w, tc=128):
    L, D, N = w.shape
    return pl.pallas_call(
        _permute_w_in_kernel,
        grid=(L, D // tc),
        in_specs=[pl.BlockSpec((1, N, tc), lambda l, i: (l, 0, i))],
        out_specs=pl.BlockSpec((1, P_TOTAL, tc), lambda l, i: (l, 0, i)),
        out_shape=jax.ShapeDtypeStruct((L, P_TOTAL, D), bf16),
        compiler_params=_cparams(2),
        name="permute_w_in",
    )(jnp.swapaxes(w, 1, 2))


def _permute_w_uq(w):
    half = MLA_ROPE // 2
    w = w.reshape(MLA_LORA, MLA_HEADS, MLA_NOPE + MLA_ROPE)
    z = jnp.zeros((MLA_LORA, MLA_HEADS, half), w.dtype)
    out = jnp.concatenate([w[..., :MLA_NOPE], w[..., MLA_NOPE:MLA_NOPE + half], z,
                           w[..., MLA_NOPE + half:], z], axis=-1)
    return out.reshape(MLA_LORA, MLA_HEADS * HEAD_PAD).astype(bf16)


def _split_w_ukv(w):
    w = w.reshape(MLA_LORA, MLA_HEADS, MLA_NOPE + MLA_V)
    wuk = w[..., :MLA_NOPE].reshape(MLA_LORA, MLA_HEADS * MLA_NOPE).astype(bf16)
    wuvt = w[..., MLA_NOPE:].reshape(MLA_LORA, MLA_HEADS * MLA_V).T.astype(bf16)
    return wuk, wuvt


def _pick(n, pref):
    return pref if n % pref == 0 else n


def kernel(x, positions, norm_g, w_in, b_merge, mla_q_norm, mla_kv_norm, mla_w_uq, mla_w_ukv,
           pool_w, pool_scale, gla_w_gate, gla_b_gate, gla_norm, w_branch, w_out, final_norm):
    B, S, D = x.shape
    assert D == D_MODEL and S % 2048 == 0
    T = B * S
    depth = norm_g.shape[0]
    blk = 512
    tq_attn = 2048
    tm_proj = _pick(T, 1024)
    tn_proj = 2304
    tg = 256
    tm_merge = 256

    xf = x.reshape(T, D)
    h, cos_t, sin_t = _prologue(xf, norm_g[0][None, :], positions.reshape(T, 1), 512)
    out = None
    w_in_p = _permute_w_in(w_in)
    for l in range(depth):
        wuq = _permute_w_uq(mla_w_uq[l])
        wuk, wuvt = _split_w_ukv(mla_w_ukv[l])
        wg = jnp.zeros((128, GLA_HEADS * GLA_DK), bf16).at[:GLA_GATE_RANK].set(
            gla_w_gate[l].astype(bf16))

        proj = _inproj(h, w_in_p, l, tm_proj, tn_proj)
        q, k, vt = _mla_up(proj, cos_t, sin_t, mla_q_norm[l][None, :], mla_kv_norm[l][None, :],
                           wuq, wuk, wuvt, B, S, blk)
        ya = _attention(q, k, vt, proj, B, S, tq_attn, blk)
        yb, yc = _pool_gla(proj, pool_w[l].astype(bf16), pool_scale[l][None, :],
                           wg, gla_b_gate[l][None, :], gla_norm[l][None, :], B, S, tg)

        final = l == depth - 1
        next_g = final_norm if final else norm_g[l + 1]
        res = _merge(ya, yb, yc, proj, b_merge[l], w_branch[l].astype(bf16),
                     w_out[l].astype(bf16), xf, next_g[None, :], tm_merge, final)
        if final:
            out = res[0]
        else:
            xf, h = res
    return out.reshape(B, S, D)
```
